```python
import math
import jax, jax.numpy as jnp
from jax import lax
import numpy as np

D_MODEL = 1024
BATCH = 2
SEQ = 8192
DEPTH = 4
DEC_BATCH = 128
DEC_SEQ = 4
PAST_LEN = 8192
PAGE_SIZE = 128

H_A = 4
KV_A = 2
G_A = H_A // KV_A
DK_A = 64
DV_A = 2 * DK_A
H_B = 4
Q_RANK = 256
KV_RANK = 128
DN_B = 128
DR_B = 64
DV_B = 128
H_C = 8
DH_C = 128
DILATED = ((128, 1), (512, 4), (2048, 16))
WIN_MAX = 2048
NUM_BUCKETS = 32
MAX_DISTANCE = 128
D_FF = 2816
CONV_W = 3

ROPE_BASE = 10000.0
Q_BLOCK = 128
EPS = 1e-6
N_AB = (DEPTH + 1) // 2
N_C = DEPTH // 2
A_OUT = H_A * DV_A
B_OUT = H_B * DV_B
AB_SIZES = (H_A * 2 * DK_A, KV_A * 2 * DK_A, KV_A * DV_A, Q_RANK, KV_RANK, DR_B)
AB_IN = H_A * 2 * DK_A + KV_A * 2 * DK_A + KV_A * DV_A + Q_RANK + KV_RANK + DR_B

kernel_name = 'hybrid_diff_mla_dilated_convffn_step'

F32 = jnp.float32


def rms_norm(x, g):
    xf = x.astype(F32)
    y = xf * lax.rsqrt(jnp.mean(xf * xf, axis=-1, keepdims=True) + EPS)
    return (y * g.astype(F32)).astype(x.dtype)


def t5_bucket(dist):
    n = jnp.maximum(dist, 0)
    max_exact = NUM_BUCKETS // 2
    nf = jnp.maximum(n, 1).astype(F32)
    large = max_exact + (jnp.log(nf / max_exact) / math.log(MAX_DISTANCE / max_exact)
                         * (NUM_BUCKETS - max_exact)).astype(jnp.int32)
    return jnp.where(n < max_exact, n, jnp.minimum(large, NUM_BUCKETS - 1))


def rope(x, pos):
    half = x.shape[-1] // 2
    inv = ROPE_BASE ** (-jnp.arange(half, dtype=F32) / half)
    ang = pos.astype(F32)[:, None] * inv[None, :]
    cos = jnp.cos(ang)[None, :, None, :]
    sin = jnp.sin(ang)[None, :, None, :]
    xf = x.astype(F32)
    x1, x2 = xf[..., :half], xf[..., half:]
    return jnp.concatenate([x1 * cos - x2 * sin, x2 * cos + x1 * sin], axis=-1).astype(x.dtype)


def map_query_blocks(fn, qs, q_pos):
    nb = q_pos.shape[0] // Q_BLOCK
    split = lambda a: jnp.moveaxis(a.reshape(a.shape[0], nb, Q_BLOCK, *a.shape[2:]), 1, 0)
    out = lax.map(lambda args: fn(*args[0], args[1]),
                  (tuple(split(a) for a in qs), q_pos.reshape(nb, Q_BLOCK)))
    out = jnp.moveaxis(out, 0, 1)
    return out.reshape(out.shape[0], nb * Q_BLOCK, *out.shape[3:])


def diff_lambda(lq1, lk1, lq2, lk2, lam_init):
    return (jnp.exp(jnp.sum(lq1.astype(F32) * lk1.astype(F32)))
            - jnp.exp(jnp.sum(lq2.astype(F32) * lk2.astype(F32))) + lam_init)


def diff_attn_core(q, k, v, q_pos, k_pos, bias_a, lam):
    n, nq = q.shape[:2]
    qg = q.reshape(n, nq, KV_A, G_A, 2, DK_A)
    s = jnp.einsum('nqkgmd,nskmd->nkgmqs', qg, k, preferred_element_type=F32) * (DK_A ** -0.5)
    dist = q_pos[:, None] - k_pos[None, :]
    bias = jnp.moveaxis(bias_a[t5_bucket(dist)], -1, 0).astype(F32)
    s = s + bias.reshape(KV_A, G_A, 1, nq, -1)
    s = jnp.where(dist >= 0, s, -jnp.inf)
    p = jax.nn.softmax(s, axis=-1)
    w = p[:, :, :, 0] - lam * p[:, :, :, 1]
    o = jnp.einsum('nkgqs,nskd->nqkgd', w.astype(v.dtype), v, preferred_element_type=F32)
    return o.reshape(n, nq, H_A, DV_A)


def mla_core(q_nope, q_rope, ckv, krope, q_pos, k_pos, w_uk, w_uv):
    q_lat = jnp.einsum('nqhd,rhd->nqhr', q_nope, w_uk)
    s = (jnp.einsum('nqhr,nkr->nhqk', q_lat, ckv, preferred_element_type=F32)
         + jnp.einsum('nqhd,nkd->nhqk', q_rope, krope, preferred_element_type=F32)) * ((DN_B + DR_B) ** -0.5)
    s = jnp.where(q_pos[:, None] >= k_pos[None, :], s, -jnp.inf)
    p = jax.nn.softmax(s, axis=-1).astype(ckv.dtype)
    o_lat = jnp.einsum('nhqk,nkr->nqhr', p, ckv)
    return jnp.einsum('nqhr,rhd->nqhd', o_lat, w_uv)


def ab_project(u, pos, w_in, g_cq, g_ckv, w_uq):
    n, s = u.shape[:2]
    z = u @ w_in
    parts, off = [], 0
    for size in AB_SIZES:
        parts.append(z[..., off:off + size])
        off += size
    qa, ka, va, cq, ckv, kr = parts
    qa = qa.reshape(n, s, H_A, 2, DK_A)
    ka = ka.reshape(n, s, KV_A, 2, DK_A)
    va = va.reshape(n, s, KV_A, DV_A)
    qb = (rms_norm(cq, g_cq) @ w_uq.reshape(Q_RANK, -1)).reshape(n, s, H_B, DN_B + DR_B)
    q_nope = qb[..., :DN_B]
    q_rope = rope(qb[..., DN_B:], pos)
    ckv = rms_norm(ckv, g_ckv)
    kr = rope(kr[:, :, None, :], pos)[:, :, 0]
    return qa, ka, va, q_nope, q_rope, ckv, kr


def ab_mixer(u, pos, past, lam_init, bias_a, w_in, lq1, lk1, lq2, lk2, g_head, g_cq, g_ckv,
             w_uq, w_uk, w_uv, w_out):
    qa, ka, va, q_nope, q_rope, ckv, kr = ab_project(u, pos, w_in, g_cq, g_ckv, w_uq)
    lam = diff_lambda(lq1, lk1, lq2, lk2, lam_init)
    if past is None:
        o_a = map_query_blocks(lambda q, qp: diff_attn_core(q, ka, va, qp, pos, bias_a, lam), (qa,), pos)
        o_b = map_query_blocks(lambda qn, qr, qp: mla_core(qn, qr, ckv, kr, qp, pos, w_uk, w_uv),
                               (q_nope, q_rope), pos)
    else:
        pk, pv, pc, pr, ppos = past
        cat = lambda a, b: jnp.concatenate([a, b], axis=1)
        k_pos = jnp.concatenate([ppos, pos])
        o_a = diff_attn_core(qa, cat(pk, ka), cat(pv, va), pos, k_pos, bias_a, lam)
        o_b = mla_core(q_nope, q_rope, cat(pc, ckv), cat(pr, kr), pos, k_pos, w_uk, w_uv)
    n, s = u.shape[:2]
    o_a = (rms_norm(o_a, g_head) * (1.0 - lam_init)).astype(u.dtype)
    o = jnp.concatenate([o_a.reshape(n, s, A_OUT), o_b.reshape(n, s, B_OUT).astype(u.dtype)], axis=-1)
    return o @ w_out, (ka, va, ckv, kr)


def combine_branches(outs, lses):
    wts = jax.nn.softmax(jnp.stack(lses), axis=0)
    return jnp.sum(wts[..., None] * jnp.stack(outs), axis=0)


def dilated_branch_prompt(q, k, v, bias_c, window, dil):
    b, s = q.shape[:2]
    band = window // dil
    blk = band
    L = s // dil
    nb = -(-L // blk)
    lp = nb * blk

    def to_class(a):
        a = a.reshape(b, L, dil, H_C, DH_C).transpose(0, 2, 1, 3, 4).reshape(b * dil, L, H_C, DH_C)
        return jnp.pad(a, ((0, 0), (0, lp - L), (0, 0), (0, 0)))

    def banded(a):
        cur = a.reshape(a.shape[0], nb, blk, H_C, DH_C)
        prev = jnp.pad(cur, ((0, 0), (1, 0), (0, 0), (0, 0), (0, 0)))[:, :-1]
        return jnp.concatenate([prev, cur], axis=2)

    qb = to_class(q).reshape(b * dil, nb, blk, H_C, DH_C)
    kb, vb = banded(to_class(k)), banded(to_class(v))
    sc = jnp.einsum('nbqhd,nbkhd->nbhqk', qb, kb, preferred_element_type=F32)
    qi = jnp.arange(blk)[:, None]
    ki = jnp.arange(2 * blk)[None, :]
    dist = blk + qi - ki
    sc = sc + jnp.moveaxis(bias_c[t5_bucket(dist * dil)], -1, 0).astype(F32)
    key_idx = jnp.arange(nb)[:, None, None] * blk - blk + ki[None]
    valid = (dist >= 0) & (dist <= band) & (key_idx >= 0)
    sc = jnp.where(valid[None, :, None], sc, -jnp.inf)
    lse = jax.nn.logsumexp(sc, axis=-1)
    p = jnp.exp(sc - lse[..., None])
    o = jnp.einsum('nbhqk,nbkhd->nbqhd', p.astype(v.dtype), vb, preferred_element_type=F32)

    def from_class(a):
        a = a.reshape(b, dil, lp, *a.shape[3:])[:, :, :L]
        return jnp.swapaxes(a, 1, 2).reshape(b, s, *a.shape[3:])

    return from_class(o), from_class(jnp.swapaxes(lse, 2, 3))


def dilated_branch_sample(q, k_all, v_all, k_base, q_pos, bias_c, window, dil):
    n, nq = q.shape[:2]
    nk = window // dil + 1
    j = jnp.arange(nk)
    idx = q_pos[:, None] - j[None, :] * dil - k_base
    valid = idx >= 0
    flat = jnp.clip(idx, 0, k_all.shape[1] - 1).reshape(-1)
    kg = jnp.take(k_all, flat, axis=1).reshape(n, nq, nk, H_C, DH_C)
    vg = jnp.take(v_all, flat, axis=1).reshape(n, nq, nk, H_C, DH_C)
    sc = jnp.einsum('nqhd,nqjhd->nhqj', q, kg, preferred_element_type=F32)
    sc = sc + jnp.moveaxis(bias_c[t5_bucket(j * dil)], -1, 0)[:, None, :].astype(F32)
    sc = jnp.where(valid[None, None], sc, -jnp.inf)
    lse = jax.nn.logsumexp(sc, axis=-1)
    p = jnp.exp(sc - lse[..., None])
    o = jnp.einsum('nhqj,nqjhd->nqhd', p.astype(vg.dtype), vg, preferred_element_type=F32)
    return o, jnp.swapaxes(lse, 1, 2)


def c_mixer(u, pos, buf, bias_c, w_in, w_out):
    n, s = u.shape[:2]
    z = (u @ w_in).reshape(n, s, 3, H_C, DH_C)
    q, k, v = z[:, :, 0] * (DH_C ** -0.5), z[:, :, 1], z[:, :, 2]
    outs, lses = [], []
    if buf is None:
        for window, dil in DILATED:
            o, l = dilated_branch_prompt(q, k, v, bias_c, window, dil)
            outs.append(o)
            lses.append(l)
    else:
        kb, vb, base = buf
        k = jnp.concatenate([kb, k], axis=1)
        v = jnp.concatenate([vb, v], axis=1)
        for window, dil in DILATED:
            o, l = dilated_branch_sample(q, k, v, base, pos, bias_c, window, dil)
            outs.append(o)
            lses.append(l)
    o = combine_branches(outs, lses).astype(u.dtype).reshape(n, s, H_C * DH_C)
    return o @ w_out, (k, v)


def conv_ffn(h, conv_prev, w_up, conv_w, conv_b, w_down):
    n, s = h.shape[:2]
    z = h @ w_up
    gate, up = z[..., :D_FF], z[..., D_FF:]
    if conv_prev is None:
        conv_prev = jnp.zeros((n, CONV_W - 1, D_FF), gate.dtype)
    ext = jnp.concatenate([conv_prev.astype(gate.dtype), gate], axis=1)
    conv = conv_b
    for tap in range(CONV_W):
        conv = conv + ext[:, tap:tap + s] * conv_w[tap]
    y = (jax.nn.silu(conv) * up) @ w_down
    return y, ext[:, -(CONV_W - 1):]


def setup_inputs(seed: int = 0) -> dict:
    key = jax.random.key(seed)
    keys = jax.random.split(key, 40)
    ks = iter([keys[i] for i in range(40)])
    nrm = lambda shape, scale=1.0: scale * jax.random.normal(next(ks), shape, jnp.float32)
    gain = lambda shape: 1.0 + nrm(shape, 0.02)
    n_pages = PAST_LEN // PAGE_SIZE
    n_used = DEC_BATCH * n_pages
    n_pool = n_used + n_used // 4
    win_buf = min(WIN_MAX, PAST_LEN)
    x_prompt = nrm((BATCH, SEQ, D_MODEL))
    x_sample = nrm((DEC_BATCH, DEC_SEQ, D_MODEL))
    cache_a_k = nrm((N_AB, n_pool, PAGE_SIZE, KV_A, 2, DK_A))
    cache_a_v = nrm((N_AB, n_pool, PAGE_SIZE, KV_A, DV_A))
    cache_mla_ckv = nrm((N_AB, n_pool, PAGE_SIZE, KV_RANK))
    cache_mla_krope = nrm((N_AB, n_pool, PAGE_SIZE, DR_B))
    state_win_k = nrm((N_C, DEC_BATCH, win_buf, H_C, DH_C))
    state_win_v = nrm((N_C, DEC_BATCH, win_buf, H_C, DH_C))
    state_conv = nrm((DEPTH, DEC_BATCH, CONV_W - 1, D_FF))
    page_table = jax.random.permutation(next(ks), n_pool)[:n_used].reshape(DEC_BATCH, n_pages).astype(jnp.int32)
    return {
        'x_prompt': x_prompt,
        'x_sample': x_sample,
        'cache_a_k': cache_a_k,
        'cache_a_v': cache_a_v,
        'cache_mla_ckv': cache_mla_ckv,
        'cache_mla_krope': cache_mla_krope,
        'state_win_k': state_win_k,
        'state_win_v': state_win_v,
        'state_conv': state_conv,
        'page_table': page_table,
        'ln_mix': gain((DEPTH, D_MODEL)),
        'ln_ffn': gain((DEPTH, D_MODEL)),
        'ln_final': gain((D_MODEL,)),
        'rel_bias': nrm((NUM_BUCKETS, H_A + H_C), 0.5),
        'w_in_ab': nrm((N_AB, D_MODEL, AB_IN), D_MODEL ** -0.5),
        'lam_q1': nrm((N_AB, DK_A), 0.1),
        'lam_k1': nrm((N_AB, DK_A), 0.1),
        'lam_q2': nrm((N_AB, DK_A), 0.1),
        'lam_k2': nrm((N_AB, DK_A), 0.1),
        'g_head_a': gain((N_AB, DV_A)),
        'g_cq': gain((N_AB, Q_RANK)),
        'g_ckv': gain((N_AB, KV_RANK)),
        'w_uq': nrm((N_AB, Q_RANK, H_B, DN_B + DR_B), Q_RANK ** -0.5),
        'w_uk': nrm((N_AB, KV_RANK, H_B, DN_B), KV_RANK ** -0.5),
        'w_uv': nrm((N_AB, KV_RANK, H_B, DV_B), KV_RANK ** -0.5),
        'w_out_ab': nrm((N_AB, A_OUT + B_OUT, D_MODEL), (A_OUT + B_OUT) ** -0.5),
        'w_in_c': nrm((N_C, D_MODEL, 3 * H_C * DH_C), D_MODEL ** -0.5),
        'w_out_c': nrm((N_C, H_C * DH_C, D_MODEL), (H_C * DH_C) ** -0.5),
        'w_up': nrm((DEPTH, D_MODEL, 2 * D_FF), D_MODEL ** -0.5),
        'conv_w': nrm((DEPTH, CONV_W, D_FF), CONV_W ** -0.5),
        'conv_b': nrm((DEPTH, D_FF), 0.01),
        'w_down': nrm((DEPTH, D_FF, D_MODEL), D_FF ** -0.5),
    }


def reference(x_prompt, x_sample, cache_a_k, cache_a_v, cache_mla_ckv, cache_mla_krope,
              state_win_k, state_win_v, state_conv, page_table,
              ln_mix, ln_ffn, ln_final, rel_bias,
              w_in_ab, lam_q1, lam_k1, lam_q2, lam_k2, g_head_a, g_cq, g_ckv, w_uq, w_uk, w_uv, w_out_ab,
              w_in_c, w_out_c, w_up, conv_w, conv_b, w_down):
    n_dec, n_pages = page_table.shape
    past = n_pages * PAGE_SIZE
    seq, dec_seq = x_prompt.shape[1], x_sample.shape[1]
    pos_p = jnp.arange(seq, dtype=jnp.int32)
    pos_s = past + jnp.arange(dec_seq, dtype=jnp.int32)
    past_pos = jnp.arange(past, dtype=jnp.int32)
    win_base = past - state_win_k.shape[2]
    keep_p = min(WIN_MAX, seq)
    keep_s = min(WIN_MAX, past + dec_seq)
    bias_a, bias_c = rel_bias[:, :H_A], rel_bias[:, H_A:]

    def paged(cache, i):
        return cache[i, page_table].reshape(n_dec, past, *cache.shape[3:])

    ak_p, ak_s, av_p, av_s, ck_p, ck_s, kr_p, kr_s = [], [], [], [], [], [], [], []
    wk_p, wk_s, wv_p, wv_s, cv_p, cv_s = [], [], [], [], [], []
    hp, hs = x_prompt, x_sample
    for li in range(DEPTH):
        i = li // 2
        up_, us_ = rms_norm(hp, ln_mix[li]), rms_norm(hs, ln_mix[li])
        if li % 2 == 0:
            lam_init = 0.8 - 0.6 * math.exp(-0.3 * li)
            w = (bias_a, w_in_ab[i], lam_q1[i], lam_k1[i], lam_q2[i], lam_k2[i], g_head_a[i],
                 g_cq[i], g_ckv[i], w_uq[i], w_uk[i], w_uv[i], w_out_ab[i])
            mp, (ka, va, ckv, kr) = ab_mixer(up_, pos_p, None, lam_init, *w)
            past_rows = (paged(cache_a_k, i), paged(cache_a_v, i), paged(cache_mla_ckv, i),
                         paged(cache_mla_krope, i), past_pos)
            ms, (ka2, va2, ckv2, kr2) = ab_mixer(us_, pos_s, past_rows, lam_init, *w)
            ak_p.append(ka); ak_s.append(ka2)
            av_p.append(va); av_s.append(va2)
            ck_p.append(ckv); ck_s.append(ckv2)
            kr_p.append(kr); kr_s.append(kr2)
        else:
            mp, (kc, vc) = c_mixer(up_, pos_p, None, bias_c, w_in_c[i], w_out_c[i])
            ms, (kc2, vc2) = c_mixer(us_, pos_s, (state_win_k[i], state_win_v[i], win_base),
                                     bias_c, w_in_c[i], w_out_c[i])
            wk_p.append(kc[:, -keep_p:]); wk_s.append(kc2[:, -keep_s:])
            wv_p.append(vc[:, -keep_p:]); wv_s.append(vc2[:, -keep_s:])
        hp, hs = hp + mp, hs + ms
        fp, cp = conv_ffn(rms_norm(hp, ln_ffn[li]), None, w_up[li], conv_w[li], conv_b[li], w_down[li])
        fs, cs = conv_ffn(rms_norm(hs, ln_ffn[li]), state_conv[li], w_up[li], conv_w[li], conv_b[li], w_down[li])
        cv_p.append(cp); cv_s.append(cs)
        hp, hs = hp + fp, hs + fs
    y_prompt = rms_norm(hp, ln_final)
    y_sample = rms_norm(hs, ln_final)
    return (y_prompt, y_sample,
            jnp.stack(ak_p), jnp.stack(ak_s), jnp.stack(av_p), jnp.stack(av_s),
            jnp.stack(ck_p), jnp.stack(ck_s), jnp.stack(kr_p), jnp.stack(kr_s),
            jnp.stack(wk_p), jnp.stack(wk_s), jnp.stack(wv_p), jnp.stack(wv_s),
            jnp.stack(cv_p), jnp.stack(cv_s))
```

```python
import functools
import math

import jax
import jax.numpy as jnp
from jax import lax
from jax.experimental import pallas as pl
from jax.experimental.pallas import tpu as pltpu

F32 = jnp.float32
BF16 = jnp.bfloat16
NEG_INF = float("-inf")

D_MODEL = 1024
H_A, KV_A, G_A, DK_A, DV_A = 4, 2, 2, 64, 128
H_B, Q_RANK, KV_RANK, DN_B, DR_B, DV_B = 4, 256, 128, 128, 64, 128
H_C, DH_C = 8, 128
DILATED = ((128, 1), (512, 4), (2048, 16))
BAND = 128
WIN_MAX = 2048
NUM_BUCKETS, MAX_DISTANCE = 32, 128
D_FF, CONV_W = 2816, 3
ROPE_BASE = 10000.0
EPS = 1e-6
PAGE_SIZE = 128

ROW_TILE = 512
FF_TILE = 1408
ATT_TILE = 256
PAGES_PER_STEP = 8
WIN_CHUNK = 512
VMEM_LIMIT = 56 * 1024 * 1024


def _cparams(n_axes, vmem=VMEM_LIMIT):
    return pltpu.CompilerParams(dimension_semantics=("arbitrary",) * n_axes, vmem_limit_bytes=vmem)


def _dot(a, b):
    return jnp.dot(a, b, preferred_element_type=F32)


def _dot_nt(a, b):
    return lax.dot_general(a, b, (((1,), (1,)), ((), ())), preferred_element_type=F32)


def _rms(x, g):
    return x * lax.rsqrt(jnp.mean(x * x, axis=-1, keepdims=True) + EPS) * g


def _bucket_starts():
    half = NUM_BUCKETS // 2
    starts = list(range(half))
    for k in range(NUM_BUCKETS - half):
        starts.append(math.ceil(half * (MAX_DISTANCE / half) ** (k / (NUM_BUCKETS - half)) - 1e-9))
    return starts


def _fill_bias(thr_ref, dist, value_of_bucket):
    def body(b, acc):
        return jnp.where(dist >= thr_ref[b], value_of_bucket(b), acc)
    init = jnp.zeros(dist.shape, F32) + value_of_bucket(0)
    return lax.fori_loop(1, NUM_BUCKETS, body, init)


def _bias_tiles_kernel(thr_ref, rb_ref, ta_ref, tc_ref, sa_ref, scb_ref, scc_ref, scbn_ref, sccn_ref,
                       *, att_tile, past, dec, win_rows, chunk):
    T = att_tile
    r = lax.broadcasted_iota(jnp.int32, (T, T), 0)
    c = lax.broadcasted_iota(jnp.int32, (T, T), 1)
    for delta in range(2):
        dist = delta * T + r - c
        for h in range(H_A):
            b = _fill_bias(thr_ref, dist, lambda k, h=h: rb_ref[k, h])
            ta_ref[h, delta] = jnp.where(dist >= 0, b, NEG_INF)
    r = lax.broadcasted_iota(jnp.int32, (BAND, 2 * BAND), 0)
    c = lax.broadcasted_iota(jnp.int32, (BAND, 2 * BAND), 1)
    dcls = BAND + r - c
    ok = (dcls >= 0) & (dcls <= BAND)
    for bi, (_, dil) in enumerate(DILATED):
        for h in range(H_C):
            b = _fill_bias(thr_ref, dcls * dil, lambda k, h=h: rb_ref[k, H_A + h])
            tc_ref[bi, h] = jnp.where(ok, b, NEG_INF)
    r = lax.broadcasted_iota(jnp.int32, (2 * dec, PAGE_SIZE), 0)
    c = lax.broadcasted_iota(jnp.int32, (2 * dec, PAGE_SIZE), 1)
    tok = r % dec
    first_head = r < dec
    for g in range(KV_A):
        val = lambda k, g=g: jnp.where(first_head, rb_ref[k, G_A * g], rb_ref[k, G_A * g + 1])
        sa_ref[0, g] = _fill_bias(thr_ref, jnp.full(r.shape, 2 * PAGE_SIZE, jnp.int32), val)
        sa_ref[1, g] = _fill_bias(thr_ref, PAGE_SIZE + tok - c, val)
        sa_ref[2, g] = jnp.where((c <= tok) & (c < dec), _fill_bias(thr_ref, tok - c, val), NEG_INF)
    def count(dist):
        n = jnp.zeros(dist.shape, F32)
        for window, dil in DILATED:
            n = n + jnp.where((dist >= 0) & (dist <= window) & (dist % dil == 0), 1.0, 0.0)
        return n
    t = lax.broadcasted_iota(jnp.int32, (dec, chunk), 0)
    c = lax.broadcasted_iota(jnp.int32, (dec, chunk), 1)
    for ch in range(win_rows // chunk):
        dist = win_rows + t - (ch * chunk + c)
        n = count(dist)
        scc_ref[ch] = n
        for h in range(H_C):
            b = _fill_bias(thr_ref, dist, lambda k, h=h: rb_ref[k, H_A + h])
            scb_ref[ch, h] = jnp.where(n > 0, b, NEG_INF)
    t = lax.broadcasted_iota(jnp.int32, (dec, PAGE_SIZE), 0)
    c = lax.broadcasted_iota(jnp.int32, (dec, PAGE_SIZE), 1)
    dist = jnp.where(c < dec, t - c, -1)
    n = count(dist)
    sccn_ref[...] = n
    for h in range(H_C):
        b = _fill_bias(thr_ref, dist, lambda k, h=h: rb_ref[k, H_A + h])
        scbn_ref[h] = jnp.where(n > 0, b, NEG_INF)


def _bias_tiles(rel_bias, past, dec, win_rows):
    T = ATT_TILE
    thr = jnp.asarray(_bucket_starts(), jnp.int32)
    nch = win_rows // WIN_CHUNK
    out_shape = (
        jax.ShapeDtypeStruct((H_A, 2, T, T), F32),
        jax.ShapeDtypeStruct((len(DILATED), H_C, BAND, 2 * BAND), F32),
        jax.ShapeDtypeStruct((3, KV_A, 2 * dec, PAGE_SIZE), F32),
        jax.ShapeDtypeStruct((nch, H_C, dec, WIN_CHUNK), F32),
        jax.ShapeDtypeStruct((nch, dec, WIN_CHUNK), F32),
        jax.ShapeDtypeStruct((H_C, dec, PAGE_SIZE), F32),
        jax.ShapeDtypeStruct((dec, PAGE_SIZE), F32),
    )
    smem = pl.BlockSpec(memory_space=pltpu.SMEM)
    return pl.pallas_call(
        functools.partial(_bias_tiles_kernel, att_tile=T, past=past, dec=dec, win_rows=win_rows, chunk=WIN_CHUNK),
        in_specs=[smem, smem],
        out_shape=out_shape,
        compiler_params=pltpu.CompilerParams(vmem_limit_bytes=VMEM_LIMIT),
        name="bias_tiles",
    )(thr, rel_bias)


def _ab_proj_kernel(h_ref, lng_ref, win_ref, gcq_ref, wuq_ref, wukt_ref, gckv_ref, cos_ref, sin_ref,
                    qa_ref, ka_ref, va_ref, ckv_ref, kr_ref, kabf_ref, vabf_ref, kcat_ref, qb_ref):
    xn = _rms(h_ref[...], lng_ref[...]).astype(BF16)
    z = _dot(xn, win_ref[...])
    qa_ref[...] = (z[:, 0:512] * (DK_A ** -0.5)).astype(BF16)
    ka = z[:, 512:768]
    va = z[:, 768:1024]
    ka_ref[...] = ka
    va_ref[...] = va
    kabf_ref[...] = ka.astype(BF16)
    vabf_ref[...] = va.astype(BF16)
    cos = cos_ref[...]
    sin = sin_ref[...]
    ckv = _rms(z[:, 1280:1408], gckv_ref[...])
    kr = z[:, 1408:1536] * cos + z[:, 1536:1664] * sin
    ckv_ref[...] = ckv
    kr_ref[...] = kr[:, 0:DR_B]
    kcat_ref[:, 0:128] = ckv.astype(BF16)
    kcat_ref[:, 128:256] = kr.astype(BF16)
    cqn = _rms(z[:, 1024:1280], gcq_ref[...]).astype(BF16)
    y = _dot(cqn, wuq_ref[...])
    scale = (DN_B + DR_B) ** -0.5
    for h in range(H_B):
        q_lat = _dot(y[:, h * 256:h * 256 + 128].astype(BF16), wukt_ref[h])
        q_rope = y[:, h * 256 + 128:h * 256 + 256] * cos + y[:, 1024 + h * 128:1152 + h * 128] * sin
        qb_ref[:, h * 256:h * 256 + 128] = (q_lat * scale).astype(BF16)
        qb_ref[:, h * 256 + 128:h * 256 + 256] = (q_rope * scale).astype(BF16)


def _ab_proj(h, lng, w_in, g_cq, w_uq, w_ukt, g_ckv, cos, sin):
    M = h.shape[0]
    tm = min(ROW_TILE, M)
    period = cos.shape[0] // tm
    row = lambda n: pl.BlockSpec((tm, n), lambda i: (i, 0))
    full = lambda a: pl.BlockSpec(a.shape, lambda i: (0,) * a.ndim)
    rot = pl.BlockSpec((tm, 128), lambda i: (i % period, 0))
    outs = ((512, BF16), (256, F32), (256, F32), (128, F32), (64, F32), (256, BF16), (256, BF16), (256, BF16),
            (1024, BF16))
    return pl.pallas_call(
        _ab_proj_kernel,
        grid=(M // tm,),
        in_specs=[row(D_MODEL), full(lng), full(w_in), full(g_cq), full(w_uq), full(w_ukt), full(g_ckv), rot, rot],
        out_specs=[row(n) for n, _ in outs],
        out_shape=[jax.ShapeDtypeStruct((M, n), dt) for n, dt in outs],
        compiler_params=_cparams(1),
        name="ab_proj",
    )(h, lng, w_in, g_cq, w_uq, w_ukt, g_ckv, cos, sin)


def _diff_lambda(lq1_ref, lk1_ref, lq2_ref, lk2_ref, lam_init):
    a = jnp.sum(lq1_ref[...] * lk1_ref[...], axis=-1, keepdims=True)
    b = jnp.sum(lq2_ref[...] * lk2_ref[...], axis=-1, keepdims=True)
    return jnp.exp(a) - jnp.exp(b) + lam_init


def _softmax_step(s, v, m_ref, l_ref, acc_ref, idx):
    m_prev = m_ref[idx]
    m_new = jnp.maximum(m_prev, jnp.max(s, axis=-1, keepdims=True))
    alpha = jnp.exp(m_prev - m_new)
    p = jnp.exp(s - m_new)
    l_ref[idx] = alpha * l_ref[idx] + jnp.sum(p, axis=-1, keepdims=True)
    acc_ref[idx] = alpha * acc_ref[idx] + _dot(p.astype(BF16), v)
    m_ref[idx] = m_new


def _flash_a_kernel(rb_ref, q_ref, k_ref, v_ref, bt_ref, lq1_ref, lk1_ref, lq2_ref, lk2_ref, gh_ref,
                    o_ref, qs_ref, m_ref, l_ref, acc_ref, *, T, lam_init):
    g = pl.program_id(1)
    qi = pl.program_id(2)
    lane = lax.broadcasted_iota(jnp.int32, (T, 128), 1)
    zero = jnp.zeros((T, 128), BF16)
    for hl in range(G_A):
        qh = q_ref[:, hl * 128:(hl + 1) * 128]
        qs_ref[2 * hl] = jnp.where(lane < DK_A, qh, zero)
        qs_ref[2 * hl + 1] = jnp.where(lane >= DK_A, qh, zero)
    m_ref[...] = jnp.full(m_ref.shape, NEG_INF, F32)
    l_ref[...] = jnp.zeros(l_ref.shape, F32)
    acc_ref[...] = jnp.zeros(acc_ref.shape, F32)

    def update(kb, bias_of_head):
        start = pl.multiple_of(kb * T, T)
        k = k_ref[pl.ds(start, T), :]
        v = v_ref[pl.ds(start, T), :]
        for c in range(2 * G_A):
            s = _dot_nt(qs_ref[c], k) + bias_of_head(c // 2)
            _softmax_step(s, v, m_ref, l_ref, acc_ref, c)

    def far(kb, carry):
        update(kb, lambda hl: rb_ref[NUM_BUCKETS - 1, G_A * g + hl])
        return carry

    lax.fori_loop(0, jnp.maximum(qi - 1, 0), far, 0)

    @pl.when(qi >= 1)
    def _():
        update(qi - 1, lambda hl: bt_ref[hl, 1])

    update(qi, lambda hl: bt_ref[hl, 0])

    lam = _diff_lambda(lq1_ref, lk1_ref, lq2_ref, lk2_ref, lam_init)
    for hl in range(G_A):
        o = acc_ref[2 * hl] / l_ref[2 * hl] - lam * (acc_ref[2 * hl + 1] / l_ref[2 * hl + 1])
        o = _rms(o, gh_ref[...]) * (1.0 - lam_init)
        o_ref[:, hl * 128:(hl + 1) * 128] = o.astype(o_ref.dtype)


def _flash_a(rel_bias, qa, ka_bf, va_bf, tiles_a, lq1, lk1, lq2, lk2, g_head, batch, seq, lam_init):
    T = ATT_TILE
    nq = seq // T
    vec = pl.BlockSpec((1, DK_A), lambda b, g, i: (0, 0))
    return pl.pallas_call(
        functools.partial(_flash_a_kernel, T=T, lam_init=lam_init),
        grid=(batch, KV_A, nq),
        in_specs=[
            pl.BlockSpec(memory_space=pltpu.SMEM),
            pl.BlockSpec((T, 256), lambda b, g, i: (b * nq + i, g)),
            pl.BlockSpec((seq, 128), lambda b, g, i: (b, g)),
            pl.BlockSpec((seq, 128), lambda b, g, i: (b, g)),
            pl.BlockSpec((G_A, 2, T, T), lambda b, g, i: (g, 0, 0, 0)),
            vec, vec, vec, vec,
            pl.BlockSpec((1, DV_A), lambda b, g, i: (0, 0)),
        ],
        out_specs=pl.BlockSpec((T, 256), lambda b, g, i: (b * nq + i, g)),
        out_shape=jax.ShapeDtypeStruct((batch * seq, H_A * DV_A), BF16),
        scratch_shapes=[
            pltpu.VMEM((2 * G_A, T, 128), BF16),
            pltpu.VMEM((2 * G_A, T, 1), F32),
            pltpu.VMEM((2 * G_A, T, 1), F32),
            pltpu.VMEM((2 * G_A, T, 128), F32),
        ],
        compiler_params=_cparams(3),
        name="flash_a",
    )(rel_bias, qa, ka_bf, va_bf, tiles_a, lq1, lk1, lq2, lk2, g_head)


def _flash_b_kernel(q_ref, kc_ref, wuv_ref, o_ref, m_ref, l_ref, acc_ref, *, T):
    qi = pl.program_id(1)
    m_ref[...] = jnp.full(m_ref.shape, NEG_INF, F32)
    l_ref[...] = jnp.zeros(l_ref.shape, F32)
    acc_ref[...] = jnp.zeros(acc_ref.shape, F32)
    r = lax.broadcasted_iota(jnp.int32, (T, T), 0)
    c = lax.broadcasted_iota(jnp.int32, (T, T), 1)

    def update(kb, diagonal):
        kc = kc_ref[pl.ds(pl.multiple_of(kb * T, T), T), :]
        v = kc[:, 0:KV_RANK]
        for h in range(H_B):
            s = _dot_nt(q_ref[:, h * 256:(h + 1) * 256], kc)
            if diagonal:
                s = jnp.where(r >= c, s, NEG_INF)
            _softmax_step(s, v, m_ref, l_ref, acc_ref, h)

    def far(kb, carry):
        update(kb, False)
        return carry

    lax.fori_loop(0, qi, far, 0)
    update(qi, True)
    for h in range(H_B):
        o_lat = (acc_ref[h] / l_ref[h]).astype(BF16)
        o_ref[:, h * 128:(h + 1) * 128] = _dot(o_lat, wuv_ref[h]).astype(o_ref.dtype)


def _flash_b(qb, kcat, w_uvt, batch, seq):
    T = ATT_TILE
    nq = seq // T
    return pl.pallas_call(
        functools.partial(_flash_b_kernel, T=T),
        grid=(batch, nq),
        in_specs=[
            pl.BlockSpec((T, H_B * 256), lambda b, i: (b * nq + i, 0)),
            pl.BlockSpec((seq, 256), lambda b, i: (b, 0)),
            pl.BlockSpec((H_B, KV_RANK, DV_B), lambda b, i: (0, 0, 0)),
        ],
        out_specs=pl.BlockSpec((T, H_B * DV_B), lambda b, i: (b * nq + i, 0)),
        out_shape=jax.ShapeDtypeStruct((batch * seq, H_B * DV_B), BF16),
        scratch_shapes=[
            pltpu.VMEM((H_B, T, 1), F32),
            pltpu.VMEM((H_B, T, 1), F32),
            pltpu.VMEM((H_B, T, 128), F32),
        ],
        compiler_params=_cparams(2),
        name="flash_b",
    )(qb, kcat, w_uvt)


def _sample_ab_kernel(pt_ref, qa_ref, qb_ref, knew_ref, vnew_ref, kcn_ref, ba_ref, lq1_ref, lk1_ref,
                      lq2_ref, lk2_ref, gh_ref, wuv_ref, *rest, P, dec, lam_init):
    kp = rest[0:P]
    vp = rest[P:2 * P]
    cp = rest[2 * P:3 * P]
    rp = rest[3 * P:4 * P]
    oa_ref, ob_ref, ma_ref, la_ref, acca_ref, mb_ref, lb_ref, accb_ref = rest[4 * P:]
    j = pl.program_id(1)
    last = j == pl.num_programs(1) - 1
    R = 2 * dec

    @pl.when(j == 0)
    def _():
        ma_ref[...] = jnp.full(ma_ref.shape, NEG_INF, F32)
        la_ref[...] = jnp.zeros(la_ref.shape, F32)
        acca_ref[...] = jnp.zeros(acca_ref.shape, F32)
        mb_ref[...] = jnp.full(mb_ref.shape, NEG_INF, F32)
        lb_ref[...] = jnp.zeros(lb_ref.shape, F32)
        accb_ref[...] = jnp.zeros(accb_ref.shape, F32)

    def online(s_list, v_list, m_ref, l_ref, acc_ref, idx):
        s = jnp.concatenate(s_list, axis=1)
        m_prev = m_ref[idx]
        m_new = jnp.maximum(m_prev, jnp.max(s, axis=-1, keepdims=True))
        alpha = jnp.exp(m_prev - m_new)
        p = jnp.exp(s - m_new)
        pv = _dot(p[:, 0:PAGE_SIZE].astype(BF16), v_list[0])
        for n in range(1, len(v_list)):
            pv = pv + _dot(p[:, n * PAGE_SIZE:(n + 1) * PAGE_SIZE].astype(BF16), v_list[n])
        l_ref[idx] = alpha * l_ref[idx] + jnp.sum(p, axis=-1, keepdims=True)
        acc_ref[idx] = alpha * acc_ref[idx] + pv
        m_ref[idx] = m_new

    for g in range(KV_A):
        v_list = [vp[n][pl.ds(g, PAGE_SIZE, stride=KV_A), :].astype(BF16) for n in range(P)]
        far = ba_ref[0, g]
        near = jnp.where(last, ba_ref[1, g], far)
        for mp in range(2):
            q = qa_ref[0, g, mp]
            s_list = []
            for n in range(P):
                s = _dot_nt(q, kp[n][:, g, mp, :].astype(BF16))
                s_list.append(s + (near if n == P - 1 else far))
            online(s_list, v_list, ma_ref, la_ref, acca_ref, 2 * g + mp)

    qb = qb_ref[0]
    c_list = [cp[n][...].astype(BF16) for n in range(P)]
    s_list = [_dot_nt(qb[:, 0:KV_RANK], c_list[n])
              + _dot_nt(qb[:, KV_RANK:KV_RANK + DR_B], rp[n][...].astype(BF16)) for n in range(P)]
    online(s_list, c_list, mb_ref, lb_ref, accb_ref, 0)

    @pl.when(last)
    def _():
        lane_a = lax.broadcasted_iota(jnp.int32, (R, PAGE_SIZE), 1)

        def new_keys(qf, keys, vals, bias, m_ref, l_ref, acc_ref, idx, lane):
            s = bias
            for t in range(dec):
                col = jnp.sum(qf * keys[t:t + 1, :], axis=-1, keepdims=True)
                s = s + jnp.where(lane == t, col, 0.0)
            m_prev = m_ref[idx]
            m_new = jnp.maximum(m_prev, jnp.max(s, axis=-1, keepdims=True))
            alpha = jnp.exp(m_prev - m_new)
            p = jnp.exp(s - m_new)
            pv = p[:, 0:1] * vals[0:1, :]
            for t in range(1, dec):
                pv = pv + p[:, t:t + 1] * vals[t:t + 1, :]
            l_ref[idx] = alpha * l_ref[idx] + jnp.sum(p, axis=-1, keepdims=True)
            acc_ref[idx] = alpha * acc_ref[idx] + pv
            m_ref[idx] = m_new

        lam = _diff_lambda(lq1_ref, lk1_ref, lq2_ref, lk2_ref, lam_init)
        for g in range(KV_A):
            for mp in range(2):
                new_keys(qa_ref[0, g, mp].astype(F32), knew_ref[0, g, mp], vnew_ref[0, g], ba_ref[2, g],
                         ma_ref, la_ref, acca_ref, 2 * g + mp, lane_a)
            o = acca_ref[2 * g] / la_ref[2 * g] - lam * (acca_ref[2 * g + 1] / la_ref[2 * g + 1])
            oa_ref[0, g] = (_rms(o, gh_ref[...]) * (1.0 - lam_init)).astype(oa_ref.dtype)

        RB = H_B * dec
        row = lax.broadcasted_iota(jnp.int32, (RB, PAGE_SIZE), 0)
        lane_b = lax.broadcasted_iota(jnp.int32, (RB, PAGE_SIZE), 1)
        bias_b = jnp.where((lane_b <= row % dec) & (lane_b < dec), 0.0, NEG_INF)
        kcn = kcn_ref[0]
        new_keys(qb.astype(F32), kcn, kcn[:, 0:KV_RANK], bias_b, mb_ref, lb_ref, accb_ref, 0, lane_b)
        o_lat = (accb_ref[0] / lb_ref[0]).astype(BF16)
        out = jnp.zeros((RB, DV_B), F32)
        for h in range(H_B):
            out = out + jnp.where(row // dec == h, _dot(o_lat, wuv_ref[h]), 0.0)
        ob_ref[0] = out.astype(ob_ref.dtype)


def _sample_ab(layer, page_table, qa, qb, knew, vnew, kcn, bias_a, lq1, lk1, lq2, lk2, g_head, w_uvt,
               cache_k, cache_v, cache_c, cache_r, dec, lam_init):
    n_seq, n_pages = page_table.shape
    P = PAGES_PER_STEP
    R = 2 * dec
    const = lambda a: pl.BlockSpec(a.shape, lambda n, j, pt: (0,) * a.ndim)
    seq_blk = lambda a: pl.BlockSpec((1,) + a.shape[1:], lambda n, j, pt: (n,) + (0,) * (a.ndim - 1))

    def page_spec(a, p):
        nd = a.ndim - 2
        return pl.BlockSpec((None, None) + a.shape[2:],
                            lambda n, j, pt: (layer, pt[n, j * P + p]) + (0,) * nd)

    pages = ([page_spec(cache_k, p) for p in range(P)] + [page_spec(cache_v, p) for p in range(P)]
             + [page_spec(cache_c, p) for p in range(P)] + [page_spec(cache_r, p) for p in range(P)])
    args = (qa, qb, knew, vnew, kcn, bias_a, lq1, lk1, lq2, lk2, g_head, w_uvt)
    in_specs = [seq_blk(qa), seq_blk(qb), seq_blk(knew), seq_blk(vnew), seq_blk(kcn), const(bias_a),
                const(lq1), const(lk1), const(lq2), const(lk2), const(g_head), const(w_uvt)] + pages
    grid_spec = pltpu.PrefetchScalarGridSpec(
        num_scalar_prefetch=1,
        grid=(n_seq, n_pages // P),
        in_specs=in_specs,
        out_specs=[pl.BlockSpec((1, KV_A, R, DV_A), lambda n, j, pt: (n, 0, 0, 0)),
                   pl.BlockSpec((1, H_B * dec, DV_B), lambda n, j, pt: (n, 0, 0))],
        scratch_shapes=[
            pltpu.VMEM((2 * KV_A, R, 1), F32), pltpu.VMEM((2 * KV_A, R, 1), F32),
            pltpu.VMEM((2 * KV_A, R, DV_A), F32),
            pltpu.VMEM((1, H_B * dec, 1), F32), pltpu.VMEM((1, H_B * dec, 1), F32),
            pltpu.VMEM((1, H_B * dec, KV_RANK), F32),
        ],
    )
    return pl.pallas_call(
        functools.partial(_sample_ab_kernel, P=P, dec=dec, lam_init=lam_init),
        grid_spec=grid_spec,
        out_shape=[jax.ShapeDtypeStruct((n_seq, KV_A, R, DV_A), BF16),
                   jax.ShapeDtypeStruct((n_seq, H_B * dec, DV_B), BF16)],
        compiler_params=_cparams(2),
        name="sample_ab",
    )(page_table, *args, *([cache_k] * P + [cache_v] * P + [cache_c] * P + [cache_r] * P))


def _out_proj_kernel(a_ref, b_ref, w_ref, res_ref, o_ref):
    ka = a_ref.shape[1]
    y = _dot(a_ref[...], w_ref[0:ka, :]) + _dot(b_ref[...], w_ref[ka:, :])
    o_ref[...] = res_ref[...] + y


def _out_proj(a, b, w, res):
    M = res.shape[0]
    tm = min(ROW_TILE, M)
    row = lambda n: pl.BlockSpec((tm, n), lambda i: (i, 0))
    return pl.pallas_call(
        _out_proj_kernel,
        grid=(M // tm,),
        in_specs=[row(a.shape[1]), row(b.shape[1]), pl.BlockSpec(w.shape, lambda i: (0, 0)), row(D_MODEL)],
        out_specs=row(D_MODEL),
        out_shape=jax.ShapeDtypeStruct((M, D_MODEL), F32),
        compiler_params=_cparams(1),
        name="out_proj",
    )(a, b, w, res)


def _c_proj_kernel(h_ref, g_ref, w_ref, q_ref, k_ref, v_ref, kbf_ref, vbf_ref):
    xn = _rms(h_ref[...], g_ref[...]).astype(BF16)
    z = _dot(xn, w_ref[...])
    n = H_C * DH_C
    q_ref[...] = (z[:, 0:n] * (DH_C ** -0.5)).astype(BF16)
    k = z[:, n:2 * n]
    v = z[:, 2 * n:3 * n]
    k_ref[...] = k
    v_ref[...] = v
    kbf_ref[...] = k.astype(BF16)
    vbf_ref[...] = v.astype(BF16)


def _c_proj(h, g, w):
    M = h.shape[0]
    tm = min(ROW_TILE, M)
    n = H_C * DH_C
    row = lambda c: pl.BlockSpec((tm, c), lambda i: (i, 0))
    dts = (BF16, F32, F32, BF16, BF16)
    return pl.pallas_call(
        _c_proj_kernel,
        grid=(M // tm,),
        in_specs=[row(D_MODEL), pl.BlockSpec(g.shape, lambda i: (0, 0)), pl.BlockSpec(w.shape, lambda i: (0, 0))],
        out_specs=[row(n) for _ in dts],
        out_shape=[jax.ShapeDtypeStruct((M, n), dt) for dt in dts],
        compiler_params=_cparams(1),
        name="c_proj",
    )(h, g, w)


def _dilated_kernel(q_ref, kp_ref, kc_ref, vp_ref, vc_ref, bt_ref, o_ref, lse_ref):
    has_prev = pl.program_id(2) > 0
    for h in range(H_C):
        sl = slice(h * DH_C, (h + 1) * DH_C)
        q = q_ref[0, :, sl]
        bt = bt_ref[h]
        sp = jnp.where(has_prev, _dot_nt(q, kp_ref[0, :, sl]) + bt[:, 0:BAND], NEG_INF)
        sc = _dot_nt(q, kc_ref[0, :, sl]) + bt[:, BAND:2 * BAND]
        m = jnp.maximum(jnp.max(sp, axis=-1, keepdims=True), jnp.max(sc, axis=-1, keepdims=True))
        pp = jnp.exp(sp - m)
        pc = jnp.exp(sc - m)
        l = jnp.sum(pp, axis=-1, keepdims=True) + jnp.sum(pc, axis=-1, keepdims=True)
        o = _dot(pp.astype(BF16), vp_ref[0, :, sl]) + _dot(pc.astype(BF16), vc_ref[0, :, sl])
        o_ref[0, :, sl] = (o / l).astype(o_ref.dtype)
        lse_ref[0, 0, :, h:h + 1] = m + jnp.log(l)


def _dilated_branch(q, k, v, tiles_c, branch, batch, seq):
    window, dil = DILATED[branch]
    assert window // dil == BAND and seq % (dil * BAND) == 0
    L = seq // dil
    nb = L // BAND
    n = H_C * DH_C
    view = lambda a: a.reshape(batch, L, dil * n)
    cur = pl.BlockSpec((1, BAND, n), lambda b, r, i: (b, i, r))
    prev = pl.BlockSpec((1, BAND, n), lambda b, r, i: (b, jnp.maximum(i - 1, 0), r))
    o, lse = pl.pallas_call(
        _dilated_kernel,
        grid=(batch, dil, nb),
        in_specs=[cur, prev, cur, prev, cur,
                  pl.BlockSpec((None, H_C, BAND, 2 * BAND), lambda b, r, i: (branch, 0, 0, 0))],
        out_specs=[cur, pl.BlockSpec((1, 1, BAND, H_C), lambda b, r, i: (b, r, i, 0))],
        out_shape=[jax.ShapeDtypeStruct((batch, L, dil * n), F32),
                   jax.ShapeDtypeStruct((batch, dil, L, H_C), F32)],
        compiler_params=_cparams(3),
        name=f"dilated_{dil}",
    )(view(q), view(k), view(k), view(v), view(v), tiles_c)
    return o.reshape(batch * seq, n), jnp.swapaxes(lse, 1, 2).reshape(batch * seq, H_C)


def _c_out_kernel(o1_ref, o2_ref, o3_ref, l1_ref, l2_ref, l3_ref, w_ref, res_ref, out_ref, mix_ref):
    ls = (l1_ref[...], l2_ref[...], l3_ref[...])
    os_ = (o1_ref, o2_ref, o3_ref)
    m = jnp.maximum(jnp.maximum(ls[0], ls[1]), ls[2])
    e = [jnp.exp(l - m) for l in ls]
    den = e[0] + e[1] + e[2]
    wts = [x / den for x in e]
    for h in range(H_C):
        sl = slice(h * DH_C, (h + 1) * DH_C)
        mix = wts[0][:, h:h + 1] * os_[0][:, sl]
        for b in range(1, 3):
            mix = mix + wts[b][:, h:h + 1] * os_[b][:, sl]
        mix_ref[:, sl] = mix.astype(BF16)
    out_ref[...] = res_ref[...] + _dot(mix_ref[...], w_ref[...])


def _c_out(outs, lses, w, res):
    M = res.shape[0]
    tm = 256
    n = H_C * DH_C
    row = lambda c: pl.BlockSpec((tm, c), lambda i: (i, 0))
    return pl.pallas_call(
        _c_out_kernel,
        grid=(M // tm,),
        in_specs=[row(n)] * 3 + [row(H_C)] * 3 + [pl.BlockSpec(w.shape, lambda i: (0, 0)), row(D_MODEL)],
        out_specs=row(D_MODEL),
        out_shape=jax.ShapeDtypeStruct((M, D_MODEL), F32),
        scratch_shapes=[pltpu.VMEM((tm, n), BF16)],
        compiler_params=_cparams(1),
        name="c_out",
    )(*outs, *lses, w, res)


def _sample_c_kernel(q_ref, knew_ref, vnew_ref, kst_ref, vst_ref, knx_ref, vnx_ref, bm_ref, cnt_ref,
                     bmn_ref, cntn_ref, prev_k_ref, prev_v_ref, o_ref, ok_ref, ov_ref, m_ref, l_ref, acc_ref,
                     *, dec, chunk):
    del prev_k_ref, prev_v_ref
    c = pl.program_id(1)
    last = c == pl.num_programs(1) - 1

    @pl.when(c == 0)
    def _():
        m_ref[...] = jnp.full(m_ref.shape, NEG_INF, F32)
        l_ref[...] = jnp.zeros(l_ref.shape, F32)
        acc_ref[...] = jnp.zeros(acc_ref.shape, F32)

    cnt = cnt_ref[0]
    for h in range(H_C):
        sl = slice(h * DH_C, (h + 1) * DH_C)
        kh = kst_ref[0, 0, :, h, :].astype(BF16)
        vh = vst_ref[0, 0, :, h, :].astype(BF16)
        s = _dot_nt(q_ref[0, :, sl], kh) + bm_ref[0, h]
        m_prev = m_ref[h]
        m_new = jnp.maximum(m_prev, jnp.max(s, axis=-1, keepdims=True))
        m_use = jnp.where(m_new == NEG_INF, 0.0, m_new)
        alpha = jnp.exp(m_prev - m_use)
        w = cnt * jnp.exp(s - m_use)
        l_ref[h] = alpha * l_ref[h] + jnp.sum(w, axis=-1, keepdims=True)
        acc_ref[h] = alpha * acc_ref[h] + _dot(w.astype(BF16), vh)
        m_ref[h] = m_new

    ok_ref[0, 0, 0:chunk - dec] = kst_ref[0, 0, dec:chunk]
    ov_ref[0, 0, 0:chunk - dec] = vst_ref[0, 0, dec:chunk]

    @pl.when(jnp.logical_not(last))
    def _():
        ok_ref[0, 0, chunk - dec:chunk] = knx_ref[0, 0]
        ov_ref[0, 0, chunk - dec:chunk] = vnx_ref[0, 0]

    @pl.when(last)
    def _():
        ok_ref[0, 0, chunk - dec:chunk] = knew_ref[0]
        ov_ref[0, 0, chunk - dec:chunk] = vnew_ref[0]
        lane = lax.broadcasted_iota(jnp.int32, (dec, PAGE_SIZE), 1)
        cntn = cntn_ref[...]
        for h in range(H_C):
            sl = slice(h * DH_C, (h + 1) * DH_C)
            qf = q_ref[0, :, sl].astype(F32)
            s = bmn_ref[h]
            for t in range(dec):
                col = jnp.sum(qf * knew_ref[0, t, h:h + 1, :], axis=-1, keepdims=True)
                s = s + jnp.where(lane == t, col, 0.0)
            m_prev = m_ref[h]
            m_new = jnp.maximum(m_prev, jnp.max(s, axis=-1, keepdims=True))
            alpha = jnp.exp(m_prev - m_new)
            w = cntn * jnp.exp(s - m_new)
            pv = w[:, 0:1] * vnew_ref[0, 0, h:h + 1, :]
            for t in range(1, dec):
                pv = pv + w[:, t:t + 1] * vnew_ref[0, t, h:h + 1, :]
            l_fin = alpha * l_ref[h] + jnp.sum(w, axis=-1, keepdims=True)
            o_ref[0, :, sl] = ((alpha * acc_ref[h] + pv) / l_fin).astype(o_ref.dtype)


def _sample_c(layer, q, knew, vnew, state_k, state_v, prev_k, prev_v, bm, cnt, bmn, cntn, dec):
    n_layers, n_seq, win_rows = state_k.shape[:3]
    chunk = WIN_CHUNK
    nch = win_rows // chunk
    n = H_C * DH_C
    seq_blk = lambda a: pl.BlockSpec((1,) + a.shape[1:], lambda s, c: (s,) + (0,) * (a.ndim - 1))
    const = lambda a: pl.BlockSpec(a.shape, lambda s, c: (0,) * a.ndim)
    st = pl.BlockSpec((1, 1, chunk, H_C, DH_C), lambda s, c: (layer, s, c, 0, 0))
    nxt = pl.BlockSpec((1, 1, dec, H_C, DH_C),
                       lambda s, c: (layer, s, jnp.minimum((c + 1) * (chunk // dec), win_rows // dec - 1), 0, 0))
    in_specs = [seq_blk(q), seq_blk(knew), seq_blk(vnew), st, st, nxt, nxt,
                pl.BlockSpec((1, H_C, dec, chunk), lambda s, c: (c, 0, 0, 0)),
                pl.BlockSpec((1, dec, chunk), lambda s, c: (c, 0, 0)), const(bmn), const(cntn)]
    args = [q, knew, vnew, state_k, state_v, state_k, state_v, bm, cnt, bmn, cntn]
    aliases = {}
    if prev_k is not None:
        in_specs += [pl.BlockSpec(memory_space=pl.ANY)] * 2
        aliases = {len(args): 1, len(args) + 1: 2}
        args += [prev_k, prev_v]
        kern = _sample_c_kernel
    else:
        kern = lambda *refs, **kw: _sample_c_kernel(*refs[:11], None, None, *refs[11:], **kw)
    win_shape = jax.ShapeDtypeStruct((n_layers, n_seq, win_rows, H_C, DH_C), F32)
    return pl.pallas_call(
        functools.partial(kern, dec=dec, chunk=chunk),
        grid=(n_seq, nch),
        in_specs=in_specs,
        out_specs=[pl.BlockSpec((1, dec, n), lambda s, c: (s, 0, 0)), st, st],
        out_shape=[jax.ShapeDtypeStruct((n_seq, dec, n), BF16), win_shape, win_shape],
        scratch_shapes=[pltpu.VMEM((H_C, dec, 1), F32), pltpu.VMEM((H_C, dec, 1), F32),
                        pltpu.VMEM((H_C, dec, DH_C), F32)],
        input_output_aliases=aliases,
        compiler_params=_cparams(2),
        name="sample_c",
    )(*args)


def _ffn_kernel(*refs, tm, nf, seq_len, final_norm):
    short = seq_len < tm
    h_ref, g_ref, wg_ref, wu_ref, cw_ref, cb_ref, wd_ref = refs[0:7]
    pos = 7
    if short:
        s1_ref, s2_ref = refs[pos:pos + 2]
        pos += 2
    if final_norm:
        gf_ref = refs[pos]
        pos += 1
    out_ref, cst_ref, xn_ref, acc_ref, ext_ref = refs[pos:pos + 5]
    carry_ref = None if short else refs[pos + 5]
    i = pl.program_id(0)
    j = pl.program_id(1)

    @pl.when(j == 0)
    def _():
        xn_ref[...] = _rms(h_ref[...], g_ref[...]).astype(BF16)
        acc_ref[...] = jnp.zeros(acc_ref.shape, F32)

    xn = xn_ref[...]
    gate = _dot(xn, wg_ref[...])
    up = _dot(xn, wu_ref[...])
    ext_ref[8:8 + tm] = gate
    if short:
        ext_ref[0:8] = jnp.zeros((8, gate.shape[1]), F32)
        cst_ref[...] = gate
    else:
        first = i % (seq_len // tm) == 0

        @pl.when(first)
        def _():
            ext_ref[0:8] = jnp.zeros((8, gate.shape[1]), F32)

        @pl.when(jnp.logical_not(first))
        def _():
            ext_ref[0:8] = carry_ref[j]

        carry_ref[j] = gate[tm - 8:tm]
        cst_ref[...] = gate[tm - 8:tm]
    g1 = ext_ref[7:7 + tm]
    g2 = ext_ref[6:6 + tm]
    if short:
        t = lax.broadcasted_iota(jnp.int32, (tm, 1), 0) % seq_len
        g1 = jnp.where(t >= 1, g1, s1_ref[...])
        g2 = jnp.where(t >= 2, g2, s2_ref[...])
    conv = cb_ref[...] + g2 * cw_ref[0:1, :] + g1 * cw_ref[1:2, :] + gate * cw_ref[2:3, :]
    y = conv * jax.nn.sigmoid(conv) * up
    acc_ref[...] += _dot(y.astype(BF16), wd_ref[...])

    @pl.when(j == nf - 1)
    def _():
        out = h_ref[...] + acc_ref[...]
        if final_norm:
            out = _rms(out, gf_ref[...])
        out_ref[...] = out


def _ffn(h, g, w_up, conv_w, conv_b, w_down, seq_len, shifted=None, final_gain=None):
    M = h.shape[0]
    tm = min(ROW_TILE, M)
    tf = FF_TILE
    nf = D_FF // tf
    short = seq_len < tm
    row = pl.BlockSpec((tm, D_MODEL), lambda i, j: (i, 0))
    in_specs = [row, pl.BlockSpec((1, D_MODEL), lambda i, j: (0, 0)),
                pl.BlockSpec((D_MODEL, tf), lambda i, j: (0, j)),
                pl.BlockSpec((D_MODEL, tf), lambda i, j: (0, nf + j)),
                pl.BlockSpec((CONV_W, tf), lambda i, j: (0, j)),
                pl.BlockSpec((1, tf), lambda i, j: (0, j)),
                pl.BlockSpec((tf, D_MODEL), lambda i, j: (j, 0))]
    args = [h, g, w_up, w_up, conv_w, conv_b, w_down]
    scratch = [pltpu.VMEM((tm, D_MODEL), BF16), pltpu.VMEM((tm, D_MODEL), F32), pltpu.VMEM((tm + 8, tf), F32)]
    if short:
        in_specs += [pl.BlockSpec((tm, tf), lambda i, j: (i, j))] * 2
        args += list(shifted)
        cst_spec = pl.BlockSpec((tm, tf), lambda i, j: (i, j))
        cst_shape = jax.ShapeDtypeStruct((M, D_FF), F32)
    else:
        cst_spec = pl.BlockSpec((None, 8, tf), lambda i, j: (i, 0, j))
        cst_shape = jax.ShapeDtypeStruct((M // tm, 8, D_FF), F32)
        scratch.append(pltpu.VMEM((nf, 8, tf), F32))
    if final_gain is not None:
        in_specs.append(pl.BlockSpec((1, D_MODEL), lambda i, j: (0, 0)))
        args.append(final_gain)
    return pl.pallas_call(
        functools.partial(_ffn_kernel, tm=tm, nf=nf, seq_len=seq_len, final_norm=final_gain is not None),
        grid=(M // tm, nf),
        in_specs=in_specs,
        out_specs=[row, cst_spec],
        out_shape=[jax.ShapeDtypeStruct((M, D_MODEL), F32), cst_shape],
        scratch_shapes=scratch,
        compiler_params=_cparams(2),
        name="ffn_short" if short else "ffn",
    )(*args)


def _rope_tables(pos):
    half = DR_B // 2
    inv = ROPE_BASE ** (-jnp.arange(half, dtype=F32) / half)
    ang = pos.astype(F32)[:, None] * inv[None, :]
    cos, sin = jnp.cos(ang), jnp.sin(ang)
    pad = jnp.zeros((pos.shape[0], 128 - DR_B), F32)
    return jnp.concatenate([cos, cos, pad], axis=1), jnp.concatenate([-sin, sin, pad], axis=1)


def _swap_halves(w):
    half = w.shape[-1] // 2
    return jnp.concatenate([w[..., half:], w[..., :half]], axis=-1)


def _prep_ab_weights(w_in, w_uq, w_uk, w_uv):
    kr = w_in[:, 1408:1472]
    z64 = jnp.zeros((D_MODEL, 64), F32)
    w_in_aug = jnp.concatenate([w_in[:, :1408], kr, z64, _swap_halves(kr), z64], axis=1).astype(BF16)
    nope = w_uq[:, :, :DN_B]
    rope = w_uq[:, :, DN_B:]
    zq = jnp.zeros((Q_RANK, H_B, 64), F32)
    main = jnp.concatenate([nope, rope, zq], axis=-1).reshape(Q_RANK, H_B * 256)
    swapped = jnp.concatenate([_swap_halves(rope), zq], axis=-1).reshape(Q_RANK, H_B * 128)
    w_uq_aug = jnp.concatenate([main, swapped], axis=1).astype(BF16)
    w_ukt = jnp.transpose(w_uk, (1, 2, 0)).astype(BF16)
    w_uvt = jnp.transpose(w_uv, (1, 0, 2)).astype(BF16)
    return w_in_aug, w_uq_aug, w_ukt, w_uvt


def kernel(x_prompt, x_sample, cache_a_k, cache_a_v, cache_mla_ckv, cache_mla_krope, state_win_k, state_win_v,
           state_conv, page_table, ln_mix, ln_ffn, ln_final, rel_bias, w_in_ab, lam_q1, lam_k1, lam_q2, lam_k2,
           g_head_a, g_cq, g_ckv, w_uq, w_uk, w_uv, w_out_ab, w_in_c, w_out_c, w_up, conv_w, conv_b, w_down):
    batch, seq, _ = x_prompt.shape
    n_seq, dec, _ = x_sample.shape
    n_pages = page_table.shape[1]
    past = n_pages * PAGE_SIZE
    depth = ln_mix.shape[0]
    win_rows = state_win_k.shape[2]
    n_pool = cache_a_k.shape[1]
    assert past == cache_a_k.shape[2] * n_pages and win_rows == WIN_MAX and past >= WIN_MAX
    assert seq >= WIN_MAX and seq % ROW_TILE == 0 and (n_seq * dec) % 8 == 0

    tiles_a, tiles_c, bias_sa, sc_bm, sc_cnt, sc_bmn, sc_cntn = _bias_tiles(rel_bias, past, dec, win_rows)
    cos_p, sin_p = _rope_tables(jnp.arange(seq, dtype=jnp.int32))
    cos_s, sin_s = _rope_tables(past + jnp.arange(dec, dtype=jnp.int32))
    cos_s, sin_s = jnp.tile(cos_s, (n_seq, 1)), jnp.tile(sin_s, (n_seq, 1))

    cache_v2 = cache_a_v.reshape(cache_a_v.shape[0], n_pool, PAGE_SIZE * KV_A, DV_A)
    hp = x_prompt.reshape(batch * seq, D_MODEL)
    hs = x_sample.reshape(n_seq * dec, D_MODEL)
    row2 = lambda a: a.reshape(1, -1)
    outs = {k: [] for k in ("ak_p", "ak_s", "av_p", "av_s", "ck_p", "ck_s", "kr_p", "kr_s", "wk_p", "wv_p",
                            "cv_p", "cv_s")}
    win_k = win_v = None

    for li in range(depth):
        i = li // 2
        lng = row2(ln_mix[li])
        if li % 2 == 0:
            lam_init = 0.8 - 0.6 * math.exp(-0.3 * li)
            w_in_aug, w_uq_aug, w_ukt, w_uvt = _prep_ab_weights(w_in_ab[i], w_uq[i], w_uk[i], w_uv[i])
            w_out = w_out_ab[i].astype(BF16)
            lam_vecs = (row2(lam_q1[i]), row2(lam_k1[i]), row2(lam_q2[i]), row2(lam_k2[i]))
            gh = row2(g_head_a[i])
            proj = lambda h, cos, sin: _ab_proj(h, lng, w_in_aug, row2(g_cq[i]), w_uq_aug, w_ukt, row2(g_ckv[i]),
                                                cos, sin)
            qa, ka, va, ckv, kr, ka_bf, va_bf, kcat, qb = proj(hp, cos_p, sin_p)
            o_a = _flash_a(rel_bias, qa, ka_bf, va_bf, tiles_a, *lam_vecs, gh, batch, seq, lam_init)
            o_b = _flash_b(qb, kcat, w_uvt, batch, seq)
            hp = _out_proj(o_a, o_b, w_out, hp)
            outs["ak_p"].append(ka.reshape(batch, seq, KV_A, 2, DK_A))
            outs["av_p"].append(va.reshape(batch, seq, KV_A, DV_A))
            outs["ck_p"].append(ckv.reshape(batch, seq, KV_RANK))
            outs["kr_p"].append(kr.reshape(batch, seq, DR_B))
            qa, ka, va, ckv, kr, _, _, _, qb = proj(hs, cos_s, sin_s)
            qa_s = jnp.transpose(qa.reshape(n_seq, dec, KV_A, G_A, 2, DK_A), (0, 2, 4, 3, 1, 5))
            qa_s = qa_s.reshape(n_seq, KV_A, 2, G_A * dec, DK_A)
            qb_s = jnp.transpose(qb.reshape(n_seq, dec, H_B, 256), (0, 2, 1, 3)).reshape(n_seq, H_B * dec, 256)
            knew = jnp.transpose(ka.reshape(n_seq, dec, KV_A, 2, DK_A), (0, 2, 3, 1, 4))
            vnew = jnp.transpose(va.reshape(n_seq, dec, KV_A, DV_A), (0, 2, 1, 3))
            kcn = jnp.concatenate([ckv, kr, jnp.zeros((n_seq * dec, 256 - KV_RANK - DR_B), F32)], axis=1)
            o_a, o_b = _sample_ab(i, page_table, qa_s, qb_s, knew, vnew, kcn.reshape(n_seq, dec, 256), bias_sa,
                                  *lam_vecs, gh, w_uvt, cache_a_k, cache_v2, cache_mla_ckv, cache_mla_krope,
                                  dec, lam_init)
            o_a = jnp.transpose(o_a.reshape(n_seq, KV_A, G_A, dec, DV_A), (0, 3, 1, 2, 4))
            o_b = jnp.transpose(o_b.reshape(n_seq, H_B, dec, DV_B), (0, 2, 1, 3))
            hs = _out_proj(o_a.reshape(n_seq * dec, H_A * DV_A), o_b.reshape(n_seq * dec, H_B * DV_B), w_out, hs)
            outs["ak_s"].append(ka.reshape(n_seq, dec, KV_A, 2, DK_A))
            outs["av_s"].append(va.reshape(n_seq, dec, KV_A, DV_A))
            outs["ck_s"].append(ckv.reshape(n_seq, dec, KV_RANK))
            outs["kr_s"].append(kr.reshape(n_seq, dec, DR_B))
        else:
            w_in = w_in_c[i].astype(BF16)
            w_out = w_out_c[i].astype(BF16)
            q, k, v, k_bf, v_bf = _c_proj(hp, lng, w_in)
            branches = [_dilated_branch(q, k_bf, v_bf, tiles_c, b, batch, seq) for b in range(len(DILATED))]
            hp = _c_out([o for o, _ in branches], [l for _, l in branches], w_out, hp)
            keep = min(WIN_MAX, seq)
            outs["wk_p"].append(k.reshape(batch, seq, H_C, DH_C)[:, seq - keep:])
            outs["wv_p"].append(v.reshape(batch, seq, H_C, DH_C)[:, seq - keep:])
            q, k, v, _, _ = _c_proj(hs, lng, w_in)
            o, win_k, win_v = _sample_c(i, q.reshape(n_seq, dec, H_C * DH_C), k.reshape(n_seq, dec, H_C, DH_C),
                                        v.reshape(n_seq, dec, H_C, DH_C), state_win_k, state_win_v, win_k, win_v,
                                        sc_bm, sc_cnt, sc_bmn, sc_cntn, dec)
            o = o.reshape(n_seq * dec, H_C * DH_C)
            half = (H_C * DH_C) // 2
            hs = _out_proj(o[:, :half], o[:, half:], w_out, hs)
        w_up_bf = w_up[li].astype(BF16)
        w_down_bf = w_down[li].astype(BF16)
        last = li == depth - 1
        fg = row2(ln_final) if last else None
        hp, cst = _ffn(hp, row2(ln_ffn[li]), w_up_bf, conv_w[li], row2(conv_b[li]), w_down_bf, seq, final_gain=fg)
        tiles = seq // ROW_TILE
        outs["cv_p"].append(cst[tiles - 1::tiles, 8 - (CONV_W - 1):])
        prev = state_conv[li]
        zrow = jnp.zeros((n_seq, 1, D_FF), F32)
        s1 = jnp.concatenate([prev[:, 1:2]] + [zrow] * (dec - 1), axis=1).reshape(n_seq * dec, D_FF)
        s2 = jnp.concatenate([prev] + [zrow] * (dec - 2), axis=1).reshape(n_seq * dec, D_FF)
        hs, gate = _ffn(hs, row2(ln_ffn[li]), w_up_bf, conv_w[li], row2(conv_b[li]), w_down_bf, dec,
                        shifted=(s1, s2), final_gain=fg)
        outs["cv_s"].append(gate.reshape(n_seq, dec, D_FF)[:, dec - (CONV_W - 1):])

    st = lambda name: jnp.stack(outs[name])
    return (hp.reshape(batch, seq, D_MODEL), hs.reshape(n_seq, dec, D_MODEL),
            st("ak_p"), st("ak_s"), st("av_p"), st("av_s"), st("ck_p"), st("ck_s"), st("kr_p"), st("kr_s"),
            st("wk_p"), win_k, st("wv_p"), win_v, st("cv_p"), st("cv_s"))
```

```python
import functools
import math

import jax
import jax.numpy as jnp
from jax import lax
from jax.experimental import pallas as pl
from jax.experimental.pallas import tpu as pltpu

F32 = jnp.float32
BF16 = jnp.bfloat16
NEG_INF = float("-inf")

D_MODEL = 1024
H_A, KV_A, G_A, DK_A, DV_A = 4, 2, 2, 64, 128
H_B, Q_RANK, KV_RANK, DN_B, DR_B, DV_B = 4, 256, 128, 128, 64, 128
H_C, DH_C = 8, 128
DILATED = ((128, 1), (512, 4), (2048, 16))
BAND = 128
WIN_MAX = 2048
NUM_BUCKETS, MAX_DISTANCE = 32, 128
D_FF, CONV_W = 2816, 3
ROPE_BASE = 10000.0
EPS = 1e-6
PAGE_SIZE = 128

ROW_TILE = 512
FF_TILE = 1408
ATT_TILE = 256
PAGES_PER_STEP = 32
WIN_CHUNK = 512
VMEM_LIMIT = 56 * 1024 * 1024


def _cparams(n_axes, vmem=VMEM_LIMIT):
    return pltpu.CompilerParams(dimension_semantics=("arbitrary",) * n_axes, vmem_limit_bytes=vmem)


def _dot(a, b):
    return jnp.dot(a, b, preferred_element_type=F32)


def _dot_nt(a, b):
    return lax.dot_general(a, b, (((1,), (1,)), ((), ())), preferred_element_type=F32)


def _rms(x, g):
    return x * lax.rsqrt(jnp.mean(x * x, axis=-1, keepdims=True) + EPS) * g


def _bucket_starts():
    half = NUM_BUCKETS // 2
    starts = list(range(half))
    for k in range(NUM_BUCKETS - half):
        starts.append(math.ceil(half * (MAX_DISTANCE / half) ** (k / (NUM_BUCKETS - half)) - 1e-9))
    return starts


def _fill_bias(thr_ref, dist, value_of_bucket):
    def body(b, acc):
        return jnp.where(dist >= thr_ref[b], value_of_bucket(b), acc)
    init = jnp.zeros(dist.shape, F32) + value_of_bucket(0)
    return lax.fori_loop(1, NUM_BUCKETS, body, init)


def _bias_tiles_kernel(thr_ref, rb_ref, ta_ref, tc_ref, sa_ref, scb_ref, scc_ref, scbn_ref, sccn_ref,
                       *, att_tile, past, dec, win_rows, chunk):
    T = att_tile
    key = lax.broadcasted_iota(jnp.int32, (T, T), 0)
    qry = lax.broadcasted_iota(jnp.int32, (T, T), 1)
    for delta in range(2):
        dist = delta * T + qry - key
        for h in range(H_A):
            b = _fill_bias(thr_ref, dist, lambda k, h=h: rb_ref[k, h]) - rb_ref[NUM_BUCKETS - 1, h]
            ta_ref[h, delta] = jnp.where(dist >= 0, b, NEG_INF)
    r = lax.broadcasted_iota(jnp.int32, (BAND, 2 * BAND), 0)
    c = lax.broadcasted_iota(jnp.int32, (BAND, 2 * BAND), 1)
    dcls = BAND + r - c
    ok = (dcls >= 0) & (dcls <= BAND)
    for bi, (_, dil) in enumerate(DILATED):
        for h in range(H_C):
            b = _fill_bias(thr_ref, dcls * dil, lambda k, h=h: rb_ref[k, H_A + h])
            tc_ref[bi, h] = jnp.where(ok, b, NEG_INF)
    r = lax.broadcasted_iota(jnp.int32, (2 * G_A * dec, PAGE_SIZE), 0)
    c = lax.broadcasted_iota(jnp.int32, (2 * G_A * dec, PAGE_SIZE), 1)
    tok = r % dec
    first_head = r % (G_A * dec) < dec
    for g in range(KV_A):
        val = lambda k, g=g: jnp.where(first_head, rb_ref[k, G_A * g], rb_ref[k, G_A * g + 1])
        sa_ref[0, g] = _fill_bias(thr_ref, jnp.full(r.shape, 2 * PAGE_SIZE, jnp.int32), val)
        sa_ref[1, g] = _fill_bias(thr_ref, PAGE_SIZE + tok - c, val)
        sa_ref[2, g] = jnp.where((c <= tok) & (c < dec), _fill_bias(thr_ref, tok - c, val), NEG_INF)
    def count(dist):
        n = jnp.zeros(dist.shape, F32)
        for window, dil in DILATED:
            n = n + jnp.where((dist >= 0) & (dist <= window) & (dist % dil == 0), 1.0, 0.0)
        return n
    head_col = lax.broadcasted_iota(jnp.int32, (H_C * dec, 1), 0) // dec

    def head_bias(k):
        col = jnp.zeros((H_C * dec, 1), F32)
        for h in range(H_C):
            col = jnp.where(head_col == h, rb_ref[k, H_A + h], col)
        return col

    def pair_tile(n_cols, row_dist):
        r = lax.broadcasted_iota(jnp.int32, (H_C * dec, n_cols), 0)
        c = lax.broadcasted_iota(jnp.int32, (H_C * dec, n_cols), 1)
        dist = row_dist(r % dec, c // H_C)
        n = jnp.where(r // dec == c % H_C, count(dist), 0.0)
        return n, jnp.where(n > 0, _fill_bias(thr_ref, dist, head_bias), NEG_INF)

    for ch in range(win_rows // chunk):
        n, b = pair_tile(chunk * H_C, lambda t, w, ch=ch: win_rows + t - (ch * chunk + w))
        scc_ref[ch] = n
        scb_ref[ch] = b
    n, b = pair_tile(dec * H_C, lambda t, w: t - w)
    sccn_ref[...] = n
    scbn_ref[...] = b


def _bias_tiles(rel_bias, past, dec, win_rows):
    T = ATT_TILE
    thr = jnp.asarray(_bucket_starts(), jnp.int32)
    nch = win_rows // WIN_CHUNK
    out_shape = (
        jax.ShapeDtypeStruct((H_A, 2, T, T), F32),
        jax.ShapeDtypeStruct((len(DILATED), H_C, BAND, 2 * BAND), F32),
        jax.ShapeDtypeStruct((3, KV_A, 2 * G_A * dec, PAGE_SIZE), F32),
        jax.ShapeDtypeStruct((nch, H_C * dec, WIN_CHUNK * H_C), F32),
        jax.ShapeDtypeStruct((nch, H_C * dec, WIN_CHUNK * H_C), F32),
        jax.ShapeDtypeStruct((H_C * dec, H_C * dec), F32),
        jax.ShapeDtypeStruct((H_C * dec, H_C * dec), F32),
    )
    smem = pl.BlockSpec(memory_space=pltpu.SMEM)
    return pl.pallas_call(
        functools.partial(_bias_tiles_kernel, att_tile=T, past=past, dec=dec, win_rows=win_rows, chunk=WIN_CHUNK),
        in_specs=[smem, smem],
        out_shape=out_shape,
        compiler_params=pltpu.CompilerParams(vmem_limit_bytes=VMEM_LIMIT),
        name="bias_tiles",
    )(thr, rel_bias)


def _ab_proj_kernel(h_ref, lng_ref, win_ref, gcq_ref, wuq_ref, wukt_ref, gckv_ref, cos_ref, sin_ref,
                    qa_ref, ka_ref, va_ref, ckv_ref, kr_ref, kabf_ref, vabf_ref, kcat_ref, qb_ref):
    xn = _rms(h_ref[...], lng_ref[...]).astype(BF16)
    z = _dot(xn, win_ref[...])
    qa_ref[...] = (z[:, 0:512] * (DK_A ** -0.5)).astype(BF16)
    ka = z[:, 512:768]
    va = z[:, 768:1024]
    ka_ref[...] = ka
    va_ref[...] = va
    kabf_ref[...] = ka.astype(BF16)
    vabf_ref[...] = va.astype(BF16)
    cos = cos_ref[...]
    sin = sin_ref[...]
    ckv = _rms(z[:, 1280:1408], gckv_ref[...])
    kr = z[:, 1408:1536] * cos + z[:, 1536:1664] * sin
    ckv_ref[...] = ckv
    kr_ref[...] = kr[:, 0:DR_B]
    kcat_ref[:, 0:128] = ckv.astype(BF16)
    kcat_ref[:, 128:256] = kr.astype(BF16)
    cqn = _rms(z[:, 1024:1280], gcq_ref[...]).astype(BF16)
    y = _dot(cqn, wuq_ref[...])
    scale = (DN_B + DR_B) ** -0.5
    for h in range(H_B):
        q_lat = _dot(y[:, h * 256:h * 256 + 128].astype(BF16), wukt_ref[h])
        q_rope = y[:, h * 256 + 128:h * 256 + 256] * cos + y[:, 1024 + h * 128:1152 + h * 128] * sin
        qb_ref[:, h * 256:h * 256 + 128] = (q_lat * scale).astype(BF16)
        qb_ref[:, h * 256 + 128:h * 256 + 256] = (q_rope * scale).astype(BF16)


def _ab_proj(h, lng, w_in, g_cq, w_uq, w_ukt, g_ckv, cos, sin):
    M = h.shape[0]
    tm = min(ROW_TILE, M)
    period = cos.shape[0] // tm
    row = lambda n: pl.BlockSpec((tm, n), lambda i: (i, 0))
    full = lambda a: pl.BlockSpec(a.shape, lambda i: (0,) * a.ndim)
    rot = pl.BlockSpec((tm, 128), lambda i: (i % period, 0))
    outs = ((512, BF16), (256, F32), (256, F32), (128, F32), (64, F32), (256, BF16), (256, BF16), (256, BF16),
            (1024, BF16))
    return pl.pallas_call(
        _ab_proj_kernel,
        grid=(M // tm,),
        in_specs=[row(D_MODEL), full(lng), full(w_in), full(g_cq), full(w_uq), full(w_ukt), full(g_ckv), rot, rot],
        out_specs=[row(n) for n, _ in outs],
        out_shape=[jax.ShapeDtypeStruct((M, n), dt) for n, dt in outs],
        compiler_params=_cparams(1),
        name="ab_proj",
    )(h, lng, w_in, g_cq, w_uq, w_ukt, g_ckv, cos, sin)


def _diff_lambda(lq1_ref, lk1_ref, lq2_ref, lk2_ref, lam_init):
    a = jnp.sum(lq1_ref[...] * lk1_ref[...], axis=-1, keepdims=True)
    b = jnp.sum(lq2_ref[...] * lk2_ref[...], axis=-1, keepdims=True)
    return jnp.exp(a) - jnp.exp(b) + lam_init


def _softmax_steps(scores, vt, m_ref, l_ref, acc_ref):
    ps, alphas = [], []
    for i, s in enumerate(scores):
        m_prev = m_ref[i]
        m_new = jnp.maximum(m_prev, jnp.max(s, axis=0, keepdims=True))
        alpha = jnp.exp(m_prev - m_new)
        p = jnp.exp(s - m_new)
        l_ref[i] = alpha * l_ref[i] + jnp.sum(p, axis=0, keepdims=True)
        m_ref[i] = m_new
        ps.append(p.astype(BF16))
        alphas.append(alpha)
    pvs = [_dot(vt, p) for p in ps]
    for i, pv in enumerate(pvs):
        acc_ref[i] = alphas[i] * acc_ref[i] + pv


def _flash_a_kernel(qt_ref, k_ref, vt_ref, bt_ref, lq1_ref, lk1_ref, lq2_ref, lk2_ref, gh_ref,
                    o_ref, m_ref, l_ref, acc_ref, *, T, lam_init):
    qi = pl.program_id(2)
    m_ref[...] = jnp.full(m_ref.shape, NEG_INF, F32)
    l_ref[...] = jnp.zeros(l_ref.shape, F32)
    acc_ref[...] = jnp.zeros(acc_ref.shape, F32)

    def update(kb, bias_of_head):
        start = pl.multiple_of(kb * T, T)
        k = k_ref[pl.ds(start, T), :]
        vt = vt_ref[:, pl.ds(start, T)]
        scores = [_dot(k, qt_ref[c]) for c in range(2 * G_A)]
        if bias_of_head is not None:
            scores = [s + bias_of_head(c // 2) for c, s in enumerate(scores)]
        _softmax_steps(scores, vt, m_ref, l_ref, acc_ref)

    def far(kb, carry):
        update(kb, None)
        return carry

    lax.fori_loop(0, jnp.maximum(qi - 1, 0), far, 0)

    @pl.when(qi >= 1)
    def _():
        update(qi - 1, lambda hl: bt_ref[hl, 1])

    update(qi, lambda hl: bt_ref[hl, 0])

    lam = _diff_lambda(lq1_ref, lk1_ref, lq2_ref, lk2_ref, lam_init)
    for hl in range(G_A):
        o = acc_ref[2 * hl] / l_ref[2 * hl] - lam * (acc_ref[2 * hl + 1] / l_ref[2 * hl + 1])
        o = o * lax.rsqrt(jnp.mean(o * o, axis=0, keepdims=True) + EPS) * gh_ref[...] * (1.0 - lam_init)
        o_ref[:, hl * 128:(hl + 1) * 128] = o.T.astype(o_ref.dtype)


def _flash_a(qa, ka_bf, va_bf, tiles_a, lq1, lk1, lq2, lk2, g_head, batch, seq, lam_init):
    T = ATT_TILE
    nq = seq // T
    q5 = jnp.transpose(qa.reshape(batch, seq, KV_A, G_A, 2 * DK_A), (0, 2, 3, 4, 1))
    first = (jnp.arange(2 * DK_A) < DK_A)[:, None]
    zero = jnp.zeros((), qa.dtype)
    qt = jnp.stack([jnp.where(first, q5, zero), jnp.where(first, zero, q5)], axis=3)
    qt = qt.reshape(batch, KV_A, 2 * G_A, 2 * DK_A, seq)
    vt = jnp.transpose(va_bf.reshape(batch, seq, KV_A, DV_A), (0, 2, 3, 1))
    vec = pl.BlockSpec((1, DK_A), lambda b, g, i: (0, 0))
    return pl.pallas_call(
        functools.partial(_flash_a_kernel, T=T, lam_init=lam_init),
        grid=(batch, KV_A, nq),
        in_specs=[
            pl.BlockSpec((None, None, 2 * G_A, 2 * DK_A, T), lambda b, g, i: (b, g, 0, 0, i)),
            pl.BlockSpec((seq, 128), lambda b, g, i: (b, g)),
            pl.BlockSpec((None, None, DV_A, seq), lambda b, g, i: (b, g, 0, 0)),
            pl.BlockSpec((G_A, 2, T, T), lambda b, g, i: (g, 0, 0, 0)),
            vec, vec, vec, vec,
            pl.BlockSpec((DV_A, 1), lambda b, g, i: (0, 0)),
        ],
        out_specs=pl.BlockSpec((T, 256), lambda b, g, i: (b * nq + i, g)),
        out_shape=jax.ShapeDtypeStruct((batch * seq, H_A * DV_A), BF16),
        scratch_shapes=[
            pltpu.VMEM((2 * G_A, 1, T), F32),
            pltpu.VMEM((2 * G_A, 1, T), F32),
            pltpu.VMEM((2 * G_A, DV_A, T), F32),
        ],
        compiler_params=_cparams(3),
        name="flash_a",
    )(qt, ka_bf, vt, tiles_a, lq1, lk1, lq2, lk2, g_head.reshape(DV_A, 1))


def _flash_b_kernel(qt_ref, kc_ref, ct_ref, wuv_ref, o_ref, m_ref, l_ref, acc_ref, *, T):
    qi = pl.program_id(1)
    m_ref[...] = jnp.full(m_ref.shape, NEG_INF, F32)
    l_ref[...] = jnp.zeros(l_ref.shape, F32)
    acc_ref[...] = jnp.zeros(acc_ref.shape, F32)
    key = lax.broadcasted_iota(jnp.int32, (T, T), 0)
    qry = lax.broadcasted_iota(jnp.int32, (T, T), 1)

    def update(kb, diagonal):
        start = pl.multiple_of(kb * T, T)
        kc = kc_ref[pl.ds(start, T), :]
        ct = ct_ref[:, pl.ds(start, T)]
        scores = [_dot(kc, qt_ref[h]) for h in range(H_B)]
        if diagonal:
            scores = [jnp.where(qry >= key, s, NEG_INF) for s in scores]
        _softmax_steps(scores, ct, m_ref, l_ref, acc_ref)

    def far(kb, carry):
        update(kb, False)
        return carry

    lax.fori_loop(0, qi, far, 0)
    update(qi, True)
    for h in range(H_B):
        o_lat = (acc_ref[h] / l_ref[h]).T.astype(BF16)
        o_ref[:, h * 128:(h + 1) * 128] = _dot(o_lat, wuv_ref[h]).astype(o_ref.dtype)


def _flash_b(qb, kcat, w_uvt, batch, seq):
    T = ATT_TILE
    nq = seq // T
    qt = jnp.transpose(qb.reshape(batch, seq, H_B, 256), (0, 2, 3, 1))
    ct = jnp.transpose(kcat[:, 0:KV_RANK].reshape(batch, seq, KV_RANK), (0, 2, 1))
    return pl.pallas_call(
        functools.partial(_flash_b_kernel, T=T),
        grid=(batch, nq),
        in_specs=[
            pl.BlockSpec((None, H_B, 256, T), lambda b, i: (b, 0, 0, i)),
            pl.BlockSpec((seq, 256), lambda b, i: (b, 0)),
            pl.BlockSpec((None, KV_RANK, seq), lambda b, i: (b, 0, 0)),
            pl.BlockSpec((H_B, KV_RANK, DV_B), lambda b, i: (0, 0, 0)),
        ],
        out_specs=pl.BlockSpec((T, H_B * DV_B), lambda b, i: (b * nq + i, 0)),
        out_shape=jax.ShapeDtypeStruct((batch * seq, H_B * DV_B), BF16),
        scratch_shapes=[
            pltpu.VMEM((H_B, 1, T), F32),
            pltpu.VMEM((H_B, 1, T), F32),
            pltpu.VMEM((H_B, KV_RANK, T), F32),
        ],
        compiler_params=_cparams(2),
        name="flash_b",
    )(qt, kcat, ct, w_uvt)


def _sample_ab_kernel(pt_ref, qa_ref, qb_ref, knew_ref, vnew_ref, kcn_ref, ba_ref, lq1_ref, lk1_ref,
                      lq2_ref, lk2_ref, gh_ref, wuv_ref, *rest, P, dec, lam_init):
    kp = rest[0:P]
    vp = rest[P:2 * P]
    cp = rest[2 * P:3 * P]
    rp = rest[3 * P:4 * P]
    oa_ref, ob_ref, ma_ref, la_ref, acca_ref, mb_ref, lb_ref, accb_ref = rest[4 * P:]
    j = pl.program_id(1)
    last = j == pl.num_programs(1) - 1
    R = 2 * dec

    @pl.when(j == 0)
    def _():
        ma_ref[...] = jnp.full(ma_ref.shape, NEG_INF, F32)
        la_ref[...] = jnp.zeros(la_ref.shape, F32)
        acca_ref[...] = jnp.zeros(acca_ref.shape, F32)
        mb_ref[...] = jnp.full(mb_ref.shape, NEG_INF, F32)
        lb_ref[...] = jnp.zeros(lb_ref.shape, F32)
        accb_ref[...] = jnp.zeros(accb_ref.shape, F32)

    def online(s_list, v_list, m_ref, l_ref, acc_ref, idx):
        s = jnp.concatenate(s_list, axis=1)
        m_prev = m_ref[idx]
        m_new = jnp.maximum(m_prev, jnp.max(s, axis=-1, keepdims=True))
        alpha = jnp.exp(m_prev - m_new)
        p = jnp.exp(s - m_new)
        pv = _dot(p[:, 0:PAGE_SIZE].astype(BF16), v_list[0])
        for n in range(1, len(v_list)):
            pv = pv + _dot(p[:, n * PAGE_SIZE:(n + 1) * PAGE_SIZE].astype(BF16), v_list[n])
        l_ref[idx] = alpha * l_ref[idx] + jnp.sum(p, axis=-1, keepdims=True)
        acc_ref[idx] = alpha * acc_ref[idx] + pv
        m_ref[idx] = m_new

    for g in range(KV_A):
        v_list = [vp[n][pl.ds(g, PAGE_SIZE, stride=KV_A), :].astype(BF16) for n in range(P)]
        far = ba_ref[0, g]
        near = jnp.where(last, ba_ref[1, g], far)
        s_list = []
        for n in range(P):
            s = jnp.concatenate([_dot(qa_ref[0, g, mp], kp[n][g, mp].astype(BF16)) for mp in range(2)], axis=0)
            s_list.append(s + (near if n == P - 1 else far))
        online(s_list, v_list, ma_ref, la_ref, acca_ref, g)

    qb = qb_ref[0]
    c_list = [cp[n][...].astype(BF16) for n in range(P)]
    s_list = [_dot_nt(qb[:, 0:KV_RANK], c_list[n])
              + _dot(qb[:, KV_RANK:KV_RANK + DR_B], rp[n][...].astype(BF16)) for n in range(P)]
    online(s_list, c_list, mb_ref, lb_ref, accb_ref, 0)

    @pl.when(last)
    def _():
        lane_a = lax.broadcasted_iota(jnp.int32, (2 * R, PAGE_SIZE), 1)
        row_a = lax.broadcasted_iota(jnp.int32, (2 * R, DK_A), 0)

        def new_keys(qf, key_row, vals, bias, m_ref, l_ref, acc_ref, idx, lane):
            s = bias
            for t in range(dec):
                col = jnp.sum(qf * key_row(t), axis=-1, keepdims=True)
                s = s + jnp.where(lane == t, col, 0.0)
            m_prev = m_ref[idx]
            m_new = jnp.maximum(m_prev, jnp.max(s, axis=-1, keepdims=True))
            alpha = jnp.exp(m_prev - m_new)
            p = jnp.exp(s - m_new)
            pv = p[:, 0:1] * vals[0:1, :]
            for t in range(1, dec):
                pv = pv + p[:, t:t + 1] * vals[t:t + 1, :]
            l_ref[idx] = alpha * l_ref[idx] + jnp.sum(p, axis=-1, keepdims=True)
            acc_ref[idx] = alpha * acc_ref[idx] + pv
            m_ref[idx] = m_new

        lam = _diff_lambda(lq1_ref, lk1_ref, lq2_ref, lk2_ref, lam_init)
        for g in range(KV_A):
            qf = jnp.concatenate([qa_ref[0, g, 0], qa_ref[0, g, 1]], axis=0).astype(F32)
            key_row = lambda t, g=g: jnp.where(row_a < R, knew_ref[0, g, 0, t:t + 1, :], knew_ref[0, g, 1, t:t + 1, :])
            new_keys(qf, key_row, vnew_ref[0, g], ba_ref[2, g], ma_ref, la_ref, acca_ref, g, lane_a)
            on = acca_ref[g] / la_ref[g]
            o = on[0:R] - lam * on[R:2 * R]
            oa_ref[0, g] = (_rms(o, gh_ref[...]) * (1.0 - lam_init)).astype(oa_ref.dtype)

        RB = H_B * dec
        row = lax.broadcasted_iota(jnp.int32, (RB, PAGE_SIZE), 0)
        lane_b = lax.broadcasted_iota(jnp.int32, (RB, PAGE_SIZE), 1)
        bias_b = jnp.where((lane_b <= row % dec) & (lane_b < dec), 0.0, NEG_INF)
        kcn = kcn_ref[0]
        new_keys(qb.astype(F32), lambda t: kcn[t:t + 1, :], kcn[:, 0:KV_RANK], bias_b, mb_ref, lb_ref, accb_ref, 0,
                 lane_b)
        o_lat = (accb_ref[0] / lb_ref[0]).astype(BF16)
        out = jnp.zeros((RB, DV_B), F32)
        for h in range(H_B):
            out = out + jnp.where(row // dec == h, _dot(o_lat, wuv_ref[h]), 0.0)
        ob_ref[0] = out.astype(ob_ref.dtype)


def _sample_ab(layer, page_table, qa, qb, knew, vnew, kcn, bias_a, lq1, lk1, lq2, lk2, g_head, w_uvt,
               cache_k, cache_v, cache_c, cache_r, dec, lam_init):
    n_seq, n_pages = page_table.shape
    P = PAGES_PER_STEP
    R = 2 * dec
    const = lambda a: pl.BlockSpec(a.shape, lambda n, j, pt: (0,) * a.ndim)
    seq_blk = lambda a: pl.BlockSpec((1,) + a.shape[1:], lambda n, j, pt: (n,) + (0,) * (a.ndim - 1))

    def page_spec(a, p):
        nd = a.ndim - 2
        return pl.BlockSpec((None, None) + a.shape[2:],
                            lambda n, j, pt: (layer, pt[n, j * P + p]) + (0,) * nd)

    pages = ([page_spec(cache_k, p) for p in range(P)] + [page_spec(cache_v, p) for p in range(P)]
             + [page_spec(cache_c, p) for p in range(P)] + [page_spec(cache_r, p) for p in range(P)])
    args = (qa, qb, knew, vnew, kcn, bias_a, lq1, lk1, lq2, lk2, g_head, w_uvt)
    in_specs = [seq_blk(qa), seq_blk(qb), seq_blk(knew), seq_blk(vnew), seq_blk(kcn), const(bias_a),
                const(lq1), const(lk1), const(lq2), const(lk2), const(g_head), const(w_uvt)] + pages
    grid_spec = pltpu.PrefetchScalarGridSpec(
        num_scalar_prefetch=1,
        grid=(n_seq, n_pages // P),
        in_specs=in_specs,
        out_specs=[pl.BlockSpec((1, KV_A, R, DV_A), lambda n, j, pt: (n, 0, 0, 0)),
                   pl.BlockSpec((1, H_B * dec, DV_B), lambda n, j, pt: (n, 0, 0))],
        scratch_shapes=[
            pltpu.VMEM((KV_A, 2 * R, 1), F32), pltpu.VMEM((KV_A, 2 * R, 1), F32),
            pltpu.VMEM((KV_A, 2 * R, DV_A), F32),
            pltpu.VMEM((1, H_B * dec, 1), F32), pltpu.VMEM((1, H_B * dec, 1), F32),
            pltpu.VMEM((1, H_B * dec, KV_RANK), F32),
        ],
    )
    return pl.pallas_call(
        functools.partial(_sample_ab_kernel, P=P, dec=dec, lam_init=lam_init),
        grid_spec=grid_spec,
        out_shape=[jax.ShapeDtypeStruct((n_seq, KV_A, R, DV_A), BF16),
                   jax.ShapeDtypeStruct((n_seq, H_B * dec, DV_B), BF16)],
        compiler_params=_cparams(2),
        name="sample_ab",
    )(page_table, *args, *([cache_k] * P + [cache_v] * P + [cache_c] * P + [cache_r] * P))


def _out_proj_kernel(a_ref, b_ref, w_ref, res_ref, o_ref):
    ka = a_ref.shape[1]
    y = _dot(a_ref[...], w_ref[0:ka, :]) + _dot(b_ref[...], w_ref[ka:, :])
    o_ref[...] = res_ref[...] + y


def _out_proj(a, b, w, res):
    M = res.shape[0]
    tm = min(ROW_TILE, M)
    row = lambda n: pl.BlockSpec((tm, n), lambda i: (i, 0))
    return pl.pallas_call(
        _out_proj_kernel,
        grid=(M // tm,),
        in_specs=[row(a.shape[1]), row(b.shape[1]), pl.BlockSpec(w.shape, lambda i: (0, 0)), row(D_MODEL)],
        out_specs=row(D_MODEL),
        out_shape=jax.ShapeDtypeStruct((M, D_MODEL), F32),
        compiler_params=_cparams(1),
        name="out_proj",
    )(a, b, w, res)


def _c_proj_kernel(h_ref, g_ref, w_ref, q_ref, k_ref, v_ref, kbf_ref, vbf_ref):
    xn = _rms(h_ref[...], g_ref[...]).astype(BF16)
    z = _dot(xn, w_ref[...])
    n = H_C * DH_C
    q_ref[...] = (z[:, 0:n] * (DH_C ** -0.5)).astype(BF16)
    k = z[:, n:2 * n]
    v = z[:, 2 * n:3 * n]
    k_ref[...] = k
    v_ref[...] = v
    kbf_ref[...] = k.astype(BF16)
    vbf_ref[...] = v.astype(BF16)


def _c_proj(h, g, w):
    M = h.shape[0]
    tm = min(ROW_TILE, M)
    n = H_C * DH_C
    row = lambda c: pl.BlockSpec((tm, c), lambda i: (i, 0))
    dts = (BF16, F32, F32, BF16, BF16)
    return pl.pallas_call(
        _c_proj_kernel,
        grid=(M // tm,),
        in_specs=[row(D_MODEL), pl.BlockSpec(g.shape, lambda i: (0, 0)), pl.BlockSpec(w.shape, lambda i: (0, 0))],
        out_specs=[row(n) for _ in dts],
        out_shape=[jax.ShapeDtypeStruct((M, n), dt) for dt in dts],
        compiler_params=_cparams(1),
        name="c_proj",
    )(h, g, w)


def _dilated_kernel(q_ref, kp_ref, kc_ref, vp_ref, vc_ref, bt_ref, o_ref, lse_ref):
    has_prev = pl.program_id(2) > 0
    for h in range(H_C):
        sl = slice(h * DH_C, (h + 1) * DH_C)
        q = q_ref[0, :, sl]
        bt = bt_ref[h]
        sp = jnp.where(has_prev, _dot_nt(q, kp_ref[0, :, sl]) + bt[:, 0:BAND], NEG_INF)
        sc = _dot_nt(q, kc_ref[0, :, sl]) + bt[:, BAND:2 * BAND]
        m = jnp.maximum(jnp.max(sp, axis=-1, keepdims=True), jnp.max(sc, axis=-1, keepdims=True))
        pp = jnp.exp(sp - m)
        pc = jnp.exp(sc - m)
        l = jnp.sum(pp, axis=-1, keepdims=True) + jnp.sum(pc, axis=-1, keepdims=True)
        o = _dot(pp.astype(BF16), vp_ref[0, :, sl]) + _dot(pc.astype(BF16), vc_ref[0, :, sl])
        o_ref[0, :, sl] = (o / l).astype(o_ref.dtype)
        lse_ref[0, 0, :, h:h + 1] = m + jnp.log(l)


def _dilated_branch(q, k, v, tiles_c, branch, batch, seq):
    window, dil = DILATED[branch]
    assert window // dil == BAND and seq % (dil * BAND) == 0
    L = seq // dil
    nb = L // BAND
    n = H_C * DH_C
    view = lambda a: a.reshape(batch, L, dil * n)
    cur = pl.BlockSpec((1, BAND, n), lambda b, r, i: (b, i, r))
    prev = pl.BlockSpec((1, BAND, n), lambda b, r, i: (b, jnp.maximum(i - 1, 0), r))
    o, lse = pl.pallas_call(
        _dilated_kernel,
        grid=(batch, dil, nb),
        in_specs=[cur, prev, cur, prev, cur,
                  pl.BlockSpec((None, H_C, BAND, 2 * BAND), lambda b, r, i: (branch, 0, 0, 0))],
        out_specs=[cur, pl.BlockSpec((1, 1, BAND, H_C), lambda b, r, i: (b, r, i, 0))],
        out_shape=[jax.ShapeDtypeStruct((batch, L, dil * n), F32),
                   jax.ShapeDtypeStruct((batch, dil, L, H_C), F32)],
        compiler_params=_cparams(3),
        name=f"dilated_{dil}",
    )(view(q), view(k), view(k), view(v), view(v), tiles_c)
    return o.reshape(batch * seq, n), jnp.swapaxes(lse, 1, 2).reshape(batch * seq, H_C)


def _c_out_kernel(o1_ref, o2_ref, o3_ref, l1_ref, l2_ref, l3_ref, w_ref, res_ref, out_ref, mix_ref):
    ls = (l1_ref[...], l2_ref[...], l3_ref[...])
    os_ = (o1_ref, o2_ref, o3_ref)
    m = jnp.maximum(jnp.maximum(ls[0], ls[1]), ls[2])
    e = [jnp.exp(l - m) for l in ls]
    den = e[0] + e[1] + e[2]
    wts = [x / den for x in e]
    for h in range(H_C):
        sl = slice(h * DH_C, (h + 1) * DH_C)
        mix = wts[0][:, h:h + 1] * os_[0][:, sl]
        for b in range(1, 3):
            mix = mix + wts[b][:, h:h + 1] * os_[b][:, sl]
        mix_ref[:, sl] = mix.astype(BF16)
    out_ref[...] = res_ref[...] + _dot(mix_ref[...], w_ref[...])


def _c_out(outs, lses, w, res):
    M = res.shape[0]
    tm = 256
    n = H_C * DH_C
    row = lambda c: pl.BlockSpec((tm, c), lambda i: (i, 0))
    return pl.pallas_call(
        _c_out_kernel,
        grid=(M // tm,),
        in_specs=[row(n)] * 3 + [row(H_C)] * 3 + [pl.BlockSpec(w.shape, lambda i: (0, 0)), row(D_MODEL)],
        out_specs=row(D_MODEL),
        out_shape=jax.ShapeDtypeStruct((M, D_MODEL), F32),
        scratch_shapes=[pltpu.VMEM((tm, n), BF16)],
        compiler_params=_cparams(1),
        name="c_out",
    )(*outs, *lses, w, res)


def _sample_c_kernel(q_ref, knew_ref, vnew_ref, kst_ref, vst_ref, knx_ref, vnx_ref, bm_ref, cnt_ref,
                     bmn_ref, cntn_ref, prev_k_ref, prev_v_ref, o_ref, ok_ref, ov_ref, m_ref, l_ref, acc_ref,
                     *, dec, chunk):
    del prev_k_ref, prev_v_ref
    c = pl.program_id(1)
    last = c == pl.num_programs(1) - 1

    @pl.when(c == 0)
    def _():
        m_ref[...] = jnp.full(m_ref.shape, NEG_INF, F32)
        l_ref[...] = jnp.zeros(l_ref.shape, F32)
        acc_ref[...] = jnp.zeros(acc_ref.shape, F32)

    q = q_ref[0]
    kf = kst_ref[0, 0].reshape(chunk * H_C, DH_C).astype(BF16)
    vf = vst_ref[0, 0].reshape(chunk * H_C, DH_C).astype(BF16)
    s = _dot_nt(q, kf) + bm_ref[c]
    m_prev = m_ref[...]
    m_new = jnp.maximum(m_prev, jnp.max(s, axis=-1, keepdims=True))
    m_use = jnp.where(m_new == NEG_INF, 0.0, m_new)
    alpha = jnp.exp(m_prev - m_use)
    w = cnt_ref[c] * jnp.exp(s - m_use)
    l_ref[...] = alpha * l_ref[...] + jnp.sum(w, axis=-1, keepdims=True)
    acc_ref[...] = alpha * acc_ref[...] + _dot(w.astype(BF16), vf)
    m_ref[...] = m_new

    ok_ref[0, 0, 0:chunk - dec] = kst_ref[0, 0, dec:chunk]
    ov_ref[0, 0, 0:chunk - dec] = vst_ref[0, 0, dec:chunk]

    @pl.when(jnp.logical_not(last))
    def _():
        ok_ref[0, 0, chunk - dec:chunk] = knx_ref[0, 0]
        ov_ref[0, 0, chunk - dec:chunk] = vnx_ref[0, 0]

    @pl.when(last)
    def _():
        ok_ref[0, 0, chunk - dec:chunk] = knew_ref[0]
        ov_ref[0, 0, chunk - dec:chunk] = vnew_ref[0]
        nk = dec * H_C
        knf = knew_ref[0].reshape(nk, DH_C).astype(BF16)
        vnf = vnew_ref[0].reshape(nk, DH_C).astype(BF16)
        sn = _dot_nt(q, knf) + bmn_ref[...]
        m_fin = jnp.maximum(m_new, jnp.max(sn, axis=-1, keepdims=True))
        a_fin = jnp.exp(m_new - m_fin)
        wn = cntn_ref[...] * jnp.exp(sn - m_fin)
        l_fin = a_fin * l_ref[...] + jnp.sum(wn, axis=-1, keepdims=True)
        o_ref[0] = ((a_fin * acc_ref[...] + _dot(wn.astype(BF16), vnf)) / l_fin).astype(o_ref.dtype)


def _sample_c(layer, q, knew, vnew, state_k, state_v, prev_k, prev_v, bm, cnt, bmn, cntn, dec):
    n_layers, n_seq, win_rows = state_k.shape[:3]
    chunk = WIN_CHUNK
    nch = win_rows // chunk
    n = H_C * DH_C
    seq_blk = lambda a: pl.BlockSpec((1,) + a.shape[1:], lambda s, c: (s,) + (0,) * (a.ndim - 1))
    const = lambda a: pl.BlockSpec(a.shape, lambda s, c: (0,) * a.ndim)
    st = pl.BlockSpec((1, 1, chunk, H_C, DH_C), lambda s, c: (layer, s, c, 0, 0))
    nxt = pl.BlockSpec((1, 1, dec, H_C, DH_C),
                       lambda s, c: (layer, s, jnp.minimum((c + 1) * (chunk // dec), win_rows // dec - 1), 0, 0))
    in_specs = [seq_blk(q), seq_blk(knew), seq_blk(vnew), st, st, nxt, nxt,
                const(bm), const(cnt), const(bmn), const(cntn)]
    args = [q, knew, vnew, state_k, state_v, state_k, state_v, bm, cnt, bmn, cntn]
    aliases = {}
    if prev_k is not None:
        in_specs += [pl.BlockSpec(memory_space=pl.ANY)] * 2
        aliases = {len(args): 1, len(args) + 1: 2}
        args += [prev_k, prev_v]
        kern = _sample_c_kernel
    else:
        kern = lambda *refs, **kw: _sample_c_kernel(*refs[:11], None, None, *refs[11:], **kw)
    win_shape = jax.ShapeDtypeStruct((n_layers, n_seq, win_rows, H_C, DH_C), F32)
    return pl.pallas_call(
        functools.partial(kern, dec=dec, chunk=chunk),
        grid=(n_seq, nch),
        in_specs=in_specs,
        out_specs=[pl.BlockSpec((1, H_C * dec, DH_C), lambda s, c: (s, 0, 0)), st, st],
        out_shape=[jax.ShapeDtypeStruct((n_seq, H_C * dec, DH_C), BF16), win_shape, win_shape],
        scratch_shapes=[pltpu.VMEM((H_C * dec, 1), F32), pltpu.VMEM((H_C * dec, 1), F32),
                        pltpu.VMEM((H_C * dec, DH_C), F32)],
        input_output_aliases=aliases,
        compiler_params=_cparams(2),
        name="sample_c",
    )(*args)


def _ffn_kernel(*refs, tm, nf, seq_len, final_norm):
    short = seq_len < tm
    h_ref, g_ref, wg_ref, wu_ref, cw_ref, cb_ref, wd_ref = refs[0:7]
    pos = 7
    if short:
        s1_ref, s2_ref = refs[pos:pos + 2]
        pos += 2
    if final_norm:
        gf_ref = refs[pos]
        pos += 1
    out_ref, cst_ref, xn_ref, acc_ref, ext_ref = refs[pos:pos + 5]
    carry_ref = None if short else refs[pos + 5]
    i = pl.program_id(0)
    j = pl.program_id(1)

    @pl.when(j == 0)
    def _():
        xn_ref[...] = _rms(h_ref[...], g_ref[...]).astype(BF16)
        acc_ref[...] = jnp.zeros(acc_ref.shape, F32)

    xn = xn_ref[...]
    gate = _dot(xn, wg_ref[...])
    up = _dot(xn, wu_ref[...])
    ext_ref[8:8 + tm] = gate
    if short:
        ext_ref[0:8] = jnp.zeros((8, gate.shape[1]), F32)
        cst_ref[...] = gate
    else:
        first = i % (seq_len // tm) == 0

        @pl.when(first)
        def _():
            ext_ref[0:8] = jnp.zeros((8, gate.shape[1]), F32)

        @pl.when(jnp.logical_not(first))
        def _():
            ext_ref[0:8] = carry_ref[j]

        carry_ref[j] = gate[tm - 8:tm]
        cst_ref[...] = gate[tm - 8:tm]
    g1 = ext_ref[7:7 + tm]
    g2 = ext_ref[6:6 + tm]
    if short:
        t = lax.broadcasted_iota(jnp.int32, (tm, 1), 0) % seq_len
        g1 = jnp.where(t >= 1, g1, s1_ref[...])
        g2 = jnp.where(t >= 2, g2, s2_ref[...])
    conv = cb_ref[...] + g2 * cw_ref[0:1, :] + g1 * cw_ref[1:2, :] + gate * cw_ref[2:3, :]
    y = conv * jax.nn.sigmoid(conv) * up
    acc_ref[...] += _dot(y.astype(BF16), wd_ref[...])

    @pl.when(j == nf - 1)
    def _():
        out = h_ref[...] + acc_ref[...]
        if final_norm:
            out = _rms(out, gf_ref[...])
        out_ref[...] = out


def _ffn(h, g, w_up, conv_w, conv_b, w_down, seq_len, shifted=None, final_gain=None):
    M = h.shape[0]
    tm = min(ROW_TILE, M)
    tf = FF_TILE
    nf = D_FF // tf
    short = seq_len < tm
    row = pl.BlockSpec((tm, D_MODEL), lambda i, j: (i, 0))
    in_specs = [row, pl.BlockSpec((1, D_MODEL), lambda i, j: (0, 0)),
                pl.BlockSpec((D_MODEL, tf), lambda i, j: (0, j)),
                pl.BlockSpec((D_MODEL, tf), lambda i, j: (0, nf + j)),
                pl.BlockSpec((CONV_W, tf), lambda i, j: (0, j)),
                pl.BlockSpec((1, tf), lambda i, j: (0, j)),
                pl.BlockSpec((tf, D_MODEL), lambda i, j: (j, 0))]
    args = [h, g, w_up, w_up, conv_w, conv_b, w_down]
    scratch = [pltpu.VMEM((tm, D_MODEL), BF16), pltpu.VMEM((tm, D_MODEL), F32), pltpu.VMEM((tm + 8, tf), F32)]
    if short:
        in_specs += [pl.BlockSpec((tm, tf), lambda i, j: (i, j))] * 2
        args += list(shifted)
        cst_spec = pl.BlockSpec((tm, tf), lambda i, j: (i, j))
        cst_shape = jax.ShapeDtypeStruct((M, D_FF), F32)
    else:
        cst_spec = pl.BlockSpec((None, 8, tf), lambda i, j: (i, 0, j))
        cst_shape = jax.ShapeDtypeStruct((M // tm, 8, D_FF), F32)
        scratch.append(pltpu.VMEM((nf, 8, tf), F32))
    if final_gain is not None:
        in_specs.append(pl.BlockSpec((1, D_MODEL), lambda i, j: (0, 0)))
        args.append(final_gain)
    return pl.pallas_call(
        functools.partial(_ffn_kernel, tm=tm, nf=nf, seq_len=seq_len, final_norm=final_gain is not None),
        grid=(M // tm, nf),
        in_specs=in_specs,
        out_specs=[row, cst_spec],
        out_shape=[jax.ShapeDtypeStruct((M, D_MODEL), F32), cst_shape],
        scratch_shapes=scratch,
        compiler_params=_cparams(2),
        name="ffn_short" if short else "ffn",
    )(*args)


def _rope_tables(pos):
    half = DR_B // 2
    inv = ROPE_BASE ** (-jnp.arange(half, dtype=F32) / half)
    ang = pos.astype(F32)[:, None] * inv[None, :]
    cos, sin = jnp.cos(ang), jnp.sin(ang)
    pad = jnp.zeros((pos.shape[0], 128 - DR_B), F32)
    return jnp.concatenate([cos, cos, pad], axis=1), jnp.concatenate([-sin, sin, pad], axis=1)


def _swap_halves(w):
    half = w.shape[-1] // 2
    return jnp.concatenate([w[..., half:], w[..., :half]], axis=-1)


def _prep_ab_weights(w_in, w_uq, w_uk, w_uv):
    kr = w_in[:, 1408:1472]
    z64 = jnp.zeros((D_MODEL, 64), F32)
    w_in_aug = jnp.concatenate([w_in[:, :1408], kr, z64, _swap_halves(kr), z64], axis=1).astype(BF16)
    nope = w_uq[:, :, :DN_B]
    rope = w_uq[:, :, DN_B:]
    zq = jnp.zeros((Q_RANK, H_B, 64), F32)
    main = jnp.concatenate([nope, rope, zq], axis=-1).reshape(Q_RANK, H_B * 256)
    swapped = jnp.concatenate([_swap_halves(rope), zq], axis=-1).reshape(Q_RANK, H_B * 128)
    w_uq_aug = jnp.concatenate([main, swapped], axis=1).astype(BF16)
    w_ukt = jnp.transpose(w_uk, (1, 2, 0)).astype(BF16)
    w_uvt = jnp.transpose(w_uv, (1, 0, 2)).astype(BF16)
    return w_in_aug, w_uq_aug, w_ukt, w_uvt


def kernel(x_prompt, x_sample, cache_a_k, cache_a_v, cache_mla_ckv, cache_mla_krope, state_win_k, state_win_v,
           state_conv, page_table, ln_mix, ln_ffn, ln_final, rel_bias, w_in_ab, lam_q1, lam_k1, lam_q2, lam_k2,
           g_head_a, g_cq, g_ckv, w_uq, w_uk, w_uv, w_out_ab, w_in_c, w_out_c, w_up, conv_w, conv_b, w_down):
    batch, seq, _ = x_prompt.shape
    n_seq, dec, _ = x_sample.shape
    n_pages = page_table.shape[1]
    past = n_pages * PAGE_SIZE
    depth = ln_mix.shape[0]
    win_rows = state_win_k.shape[2]
    n_pool = cache_a_k.shape[1]
    assert past == cache_a_k.shape[2] * n_pages and win_rows == WIN_MAX and past >= WIN_MAX
    assert seq >= WIN_MAX and seq % ROW_TILE == 0 and (n_seq * dec) % 8 == 0

    tiles_a, tiles_c, bias_sa, sc_bm, sc_cnt, sc_bmn, sc_cntn = _bias_tiles(rel_bias, past, dec, win_rows)
    cos_p, sin_p = _rope_tables(jnp.arange(seq, dtype=jnp.int32))
    cos_s, sin_s = _rope_tables(past + jnp.arange(dec, dtype=jnp.int32))
    cos_s, sin_s = jnp.tile(cos_s, (n_seq, 1)), jnp.tile(sin_s, (n_seq, 1))

    cache_kt = jnp.transpose(cache_a_k, (0, 1, 3, 4, 5, 2))
    cache_rt = jnp.transpose(cache_mla_krope, (0, 1, 3, 2))
    cache_v2 = cache_a_v.reshape(cache_a_v.shape[0], n_pool, PAGE_SIZE * KV_A, DV_A)
    hp = x_prompt.reshape(batch * seq, D_MODEL)
    hs = x_sample.reshape(n_seq * dec, D_MODEL)
    row2 = lambda a: a.reshape(1, -1)
    outs = {k: [] for k in ("ak_p", "ak_s", "av_p", "av_s", "ck_p", "ck_s", "kr_p", "kr_s", "wk_p", "wv_p",
                            "cv_p", "cv_s")}
    win_k = win_v = None

    for li in range(depth):
        i = li // 2
        lng = row2(ln_mix[li])
        if li % 2 == 0:
            lam_init = 0.8 - 0.6 * math.exp(-0.3 * li)
            w_in_aug, w_uq_aug, w_ukt, w_uvt = _prep_ab_weights(w_in_ab[i], w_uq[i], w_uk[i], w_uv[i])
            w_out = w_out_ab[i].astype(BF16)
            lam_vecs = (row2(lam_q1[i]), row2(lam_k1[i]), row2(lam_q2[i]), row2(lam_k2[i]))
            gh = row2(g_head_a[i])
            proj = lambda h, cos, sin: _ab_proj(h, lng, w_in_aug, row2(g_cq[i]), w_uq_aug, w_ukt, row2(g_ckv[i]),
                                                cos, sin)
            qa, ka, va, ckv, kr, ka_bf, va_bf, kcat, qb = proj(hp, cos_p, sin_p)
            o_a = _flash_a(qa, ka_bf, va_bf, tiles_a, *lam_vecs, gh, batch, seq, lam_init)
            o_b = _flash_b(qb, kcat, w_uvt, batch, seq)
            hp = _out_proj(o_a, o_b, w_out, hp)
            outs["ak_p"].append(ka.reshape(batch, seq, KV_A, 2, DK_A))
            outs["av_p"].append(va.reshape(batch, seq, KV_A, DV_A))
            outs["ck_p"].append(ckv.reshape(batch, seq, KV_RANK))
            outs["kr_p"].append(kr.reshape(batch, seq, DR_B))
            qa, ka, va, ckv, kr, _, _, _, qb = proj(hs, cos_s, sin_s)
            qa_s = jnp.transpose(qa.reshape(n_seq, dec, KV_A, G_A, 2, DK_A), (0, 2, 4, 3, 1, 5))
            qa_s = qa_s.reshape(n_seq, KV_A, 2, G_A * dec, DK_A)
            qb_s = jnp.transpose(qb.reshape(n_seq, dec, H_B, 256), (0, 2, 1, 3)).reshape(n_seq, H_B * dec, 256)
            knew = jnp.transpose(ka.reshape(n_seq, dec, KV_A, 2, DK_A), (0, 2, 3, 1, 4))
            vnew = jnp.transpose(va.reshape(n_seq, dec, KV_A, DV_A), (0, 2, 1, 3))
            kcn = jnp.concatenate([ckv, kr, jnp.zeros((n_seq * dec, 256 - KV_RANK - DR_B), F32)], axis=1)
            o_a, o_b = _sample_ab(i, page_table, qa_s, qb_s, knew, vnew, kcn.reshape(n_seq, dec, 256), bias_sa,
                                  *lam_vecs, gh, w_uvt, cache_kt, cache_v2, cache_mla_ckv, cache_rt, dec, lam_init)
            o_a = jnp.transpose(o_a.reshape(n_seq, KV_A, G_A, dec, DV_A), (0, 3, 1, 2, 4))
            o_b = jnp.transpose(o_b.reshape(n_seq, H_B, dec, DV_B), (0, 2, 1, 3))
            hs = _out_proj(o_a.reshape(n_seq * dec, H_A * DV_A), o_b.reshape(n_seq * dec, H_B * DV_B), w_out, hs)
            outs["ak_s"].append(ka.reshape(n_seq, dec, KV_A, 2, DK_A))
            outs["av_s"].append(va.reshape(n_seq, dec, KV_A, DV_A))
            outs["ck_s"].append(ckv.reshape(n_seq, dec, KV_RANK))
            outs["kr_s"].append(kr.reshape(n_seq, dec, DR_B))
        else:
            w_in = w_in_c[i].astype(BF16)
            w_out = w_out_c[i].astype(BF16)
            q, k, v, k_bf, v_bf = _c_proj(hp, lng, w_in)
            branches = [_dilated_branch(q, k_bf, v_bf, tiles_c, b, batch, seq) for b in range(len(DILATED))]
            hp = _c_out([o for o, _ in branches], [l for _, l in branches], w_out, hp)
            keep = min(WIN_MAX, seq)
            outs["wk_p"].append(k.reshape(batch, seq, H_C, DH_C)[:, seq - keep:])
            outs["wv_p"].append(v.reshape(batch, seq, H_C, DH_C)[:, seq - keep:])
            q, k, v, _, _ = _c_proj(hs, lng, w_in)
            q_s = jnp.transpose(q.reshape(n_seq, dec, H_C, DH_C), (0, 2, 1, 3)).reshape(n_seq, H_C * dec, DH_C)
            o, win_k, win_v = _sample_c(i, q_s, k.reshape(n_seq, dec, H_C, DH_C), v.reshape(n_seq, dec, H_C, DH_C),
                                        state_win_k, state_win_v, win_k, win_v, sc_bm, sc_cnt, sc_bmn, sc_cntn, dec)
            o = jnp.transpose(o.reshape(n_seq, H_C, dec, DH_C), (0, 2, 1, 3)).reshape(n_seq * dec, H_C * DH_C)
            half = (H_C * DH_C) // 2
            hs = _out_proj(o[:, :half], o[:, half:], w_out, hs)
        w_up_bf = w_up[li].astype(BF16)
        w_down_bf = w_down[li].astype(BF16)
        last = li == depth - 1
        fg = row2(ln_final) if last else None
        hp, cst = _ffn(hp, row2(ln_ffn[li]), w_up_bf, conv_w[li], row2(conv_b[li]), w_down_bf, seq, final_gain=fg)
        tiles = seq // ROW_TILE
        outs["cv_p"].append(cst[tiles - 1::tiles, 8 - (CONV_W - 1):])
        prev = state_conv[li]
        zrow = jnp.zeros((n_seq, 1, D_FF), F32)
        s1 = jnp.concatenate([prev[:, 1:2]] + [zrow] * (dec - 1), axis=1).reshape(n_seq * dec, D_FF)
        s2 = jnp.concatenate([prev] + [zrow] * (dec - 2), axis=1).reshape(n_seq * dec, D_FF)
        hs, gate = _ffn(hs, row2(ln_ffn[li]), w_up_bf, conv_w[li], row2(conv_b[li]), w_down_bf, dec,
                        shifted=(s1, s2), final_gain=fg)
        outs["cv_s"].append(gate.reshape(n_seq, dec, D_FF)[:, dec - (CONV_W - 1):])

    st = lambda name: jnp.stack(outs[name])
    return (hp.reshape(batch, seq, D_MODEL), hs.reshape(n_seq, dec, D_MODEL),
            st("ak_p"), st("ak_s"), st("av_p"), st("av_s"), st("ck_p"), st("ck_s"), st("kr_p"), st("kr_s"),
            st("wk_p"), win_k, st("wv_p"), win_v, st("cv_p"), st("cv_s"))
```

```python
import functools
import math

import jax
import jax.numpy as jnp
from jax import lax
from jax.experimental import pallas as pl
from jax.experimental.pallas import tpu as pltpu

F32 = jnp.float32
BF16 = jnp.bfloat16
NEG_INF = float("-inf")

D_MODEL = 1024
H_A, KV_A, G_A, DK_A, DV_A = 4, 2, 2, 64, 128
H_B, Q_RANK, KV_RANK, DN_B, DR_B, DV_B = 4, 256, 128, 128, 64, 128
H_C, DH_C = 8, 128
DILATED = ((128, 1), (512, 4), (2048, 16))
BAND = 128
LSE_W = 128 // H_C
WIN_MAX = 2048
NUM_BUCKETS, MAX_DISTANCE = 32, 128
D_FF, CONV_W = 2816, 3
ROPE_BASE = 10000.0
EPS = 1e-6
PAGE_SIZE = 128

ROW_TILE = 512
FF_TILE = 1408
FF_GROUP = 768
ATT_TILE = 512
LOG2E = math.log2(math.e)
PAGES_PER_STEP = 32
WIN_CHUNK = 512
VMEM_LIMIT = 56 * 1024 * 1024


def _cparams(n_axes, vmem=VMEM_LIMIT):
    return pltpu.CompilerParams(dimension_semantics=("arbitrary",) * n_axes, vmem_limit_bytes=vmem)


def _dot(a, b):
    return jnp.dot(a, b, preferred_element_type=F32)


def _dot_nt(a, b):
    return lax.dot_general(a, b, (((1,), (1,)), ((), ())), preferred_element_type=F32)


def _rms(x, g):
    return x * lax.rsqrt(jnp.mean(x * x, axis=-1, keepdims=True) + EPS) * g


def _bucket_starts():
    half = NUM_BUCKETS // 2
    starts = list(range(half))
    for k in range(NUM_BUCKETS - half):
        starts.append(math.ceil(half * (MAX_DISTANCE / half) ** (k / (NUM_BUCKETS - half)) - 1e-9))
    return starts


def _fill_bias(thr_ref, dist, value_of_bucket):
    def body(b, acc):
        return jnp.where(dist >= thr_ref[b], value_of_bucket(b), acc)
    init = jnp.zeros(dist.shape, F32) + value_of_bucket(0)
    return lax.fori_loop(1, NUM_BUCKETS, body, init)


def _bias_tiles_kernel(thr_ref, rb_ref, ta_ref, tc_ref, sa_ref, scb_ref, scc_ref, scbn_ref, sccn_ref,
                       *, att_tile, past, dec, win_rows, chunk):
    T = att_tile
    key = lax.broadcasted_iota(jnp.int32, (T, T), 0)
    qry = lax.broadcasted_iota(jnp.int32, (T, T), 1)
    for delta in range(2):
        dist = delta * T + qry - key
        for h in range(H_A):
            b = _fill_bias(thr_ref, dist, lambda k, h=h: rb_ref[k, h]) - rb_ref[NUM_BUCKETS - 1, h]
            ta_ref[h, delta] = jnp.where(dist >= 0, b * LOG2E, NEG_INF)
    r = lax.broadcasted_iota(jnp.int32, (BAND, 2 * BAND), 0)
    c = lax.broadcasted_iota(jnp.int32, (BAND, 2 * BAND), 1)
    dcls = BAND + r - c
    ok = (dcls >= 0) & (dcls <= BAND)
    for bi, (_, dil) in enumerate(DILATED):
        for h in range(H_C):
            b = _fill_bias(thr_ref, dcls * dil, lambda k, h=h: rb_ref[k, H_A + h])
            tc_ref[bi, h] = jnp.where(ok, b, NEG_INF)
    r = lax.broadcasted_iota(jnp.int32, (2 * G_A * dec, PAGE_SIZE), 0)
    c = lax.broadcasted_iota(jnp.int32, (2 * G_A * dec, PAGE_SIZE), 1)
    tok = r % dec
    first_head = r % (G_A * dec) < dec
    for g in range(KV_A):
        val = lambda k, g=g: jnp.where(first_head, rb_ref[k, G_A * g], rb_ref[k, G_A * g + 1])
        sa_ref[0, g] = _fill_bias(thr_ref, jnp.full(r.shape, 2 * PAGE_SIZE, jnp.int32), val)
        sa_ref[1, g] = _fill_bias(thr_ref, PAGE_SIZE + tok - c, val)
        sa_ref[2, g] = jnp.where((c <= tok) & (c < dec), _fill_bias(thr_ref, tok - c, val), NEG_INF)
    def count(dist):
        n = jnp.zeros(dist.shape, F32)
        for window, dil in DILATED:
            n = n + jnp.where((dist >= 0) & (dist <= window) & (dist % dil == 0), 1.0, 0.0)
        return n
    head_col = lax.broadcasted_iota(jnp.int32, (H_C * dec, 1), 0) // dec

    def head_bias(k):
        col = jnp.zeros((H_C * dec, 1), F32)
        for h in range(H_C):
            col = jnp.where(head_col == h, rb_ref[k, H_A + h], col)
        return col

    def pair_tile(n_cols, row_dist):
        r = lax.broadcasted_iota(jnp.int32, (H_C * dec, n_cols), 0)
        c = lax.broadcasted_iota(jnp.int32, (H_C * dec, n_cols), 1)
        dist = row_dist(r % dec, c // H_C)
        n = jnp.where(r // dec == c % H_C, count(dist), 0.0)
        return n, jnp.where(n > 0, _fill_bias(thr_ref, dist, head_bias), NEG_INF)

    for ch in range(win_rows // chunk):
        n, b = pair_tile(chunk * H_C, lambda t, w, ch=ch: win_rows + t - (ch * chunk + w))
        scc_ref[ch] = n
        scb_ref[ch] = b
    n, b = pair_tile(dec * H_C, lambda t, w: t - w)
    sccn_ref[...] = n
    scbn_ref[...] = b


def _bias_tiles(rel_bias, past, dec, win_rows):
    T = ATT_TILE
    thr = jnp.asarray(_bucket_starts(), jnp.int32)
    nch = win_rows // WIN_CHUNK
    out_shape = (
        jax.ShapeDtypeStruct((H_A, 2, T, T), F32),
        jax.ShapeDtypeStruct((len(DILATED), H_C, BAND, 2 * BAND), F32),
        jax.ShapeDtypeStruct((3, KV_A, 2 * G_A * dec, PAGE_SIZE), F32),
        jax.ShapeDtypeStruct((nch, H_C * dec, WIN_CHUNK * H_C), F32),
        jax.ShapeDtypeStruct((nch, H_C * dec, WIN_CHUNK * H_C), F32),
        jax.ShapeDtypeStruct((H_C * dec, H_C * dec), F32),
        jax.ShapeDtypeStruct((H_C * dec, H_C * dec), F32),
    )
    smem = pl.BlockSpec(memory_space=pltpu.SMEM)
    return pl.pallas_call(
        functools.partial(_bias_tiles_kernel, att_tile=T, past=past, dec=dec, win_rows=win_rows, chunk=WIN_CHUNK),
        in_specs=[smem, smem],
        out_shape=out_shape,
        compiler_params=pltpu.CompilerParams(vmem_limit_bytes=VMEM_LIMIT),
        name="bias_tiles",
    )(thr, rel_bias)


def _ab_proj_kernel(h_ref, lng_ref, win_ref, gcq_ref, wuq_ref, wukt_ref, gckv_ref, cos_ref, sin_ref,
                    qa_ref, ka_ref, va_ref, ckv_ref, kr_ref, kabf_ref, vabf_ref, kcat_ref, qb_ref, *, q_mult):
    xn = _rms(h_ref[...], lng_ref[...]).astype(BF16)
    z = _dot(xn, win_ref[...])
    qa_ref[...] = (z[:, 0:512] * (DK_A ** -0.5 * q_mult)).astype(BF16)
    ka = z[:, 512:768]
    va = z[:, 768:1024]
    ka_ref[...] = ka
    va_ref[...] = va
    kabf_ref[...] = ka.astype(BF16)
    vabf_ref[...] = va.astype(BF16)
    cos = cos_ref[...]
    sin = sin_ref[...]
    ckv = _rms(z[:, 1280:1408], gckv_ref[...])
    kr = z[:, 1408:1536] * cos + z[:, 1536:1664] * sin
    ckv_ref[...] = ckv
    kr_ref[...] = kr[:, 0:DR_B]
    kcat_ref[:, 0:128] = ckv.astype(BF16)
    kcat_ref[:, 128:256] = kr.astype(BF16)
    cqn = _rms(z[:, 1024:1280], gcq_ref[...]).astype(BF16)
    y = _dot(cqn, wuq_ref[...])
    scale = (DN_B + DR_B) ** -0.5 * q_mult
    for h in range(H_B):
        q_lat = _dot(y[:, h * 256:h * 256 + 128].astype(BF16), wukt_ref[h])
        q_rope = y[:, h * 256 + 128:h * 256 + 256] * cos + y[:, 1024 + h * 128:1152 + h * 128] * sin
        qb_ref[:, h * 256:h * 256 + 128] = (q_lat * scale).astype(BF16)
        qb_ref[:, h * 256 + 128:h * 256 + 256] = (q_rope * scale).astype(BF16)


def _ab_proj(h, lng, w_in, g_cq, w_uq, w_ukt, g_ckv, cos, sin, q_mult):
    M = h.shape[0]
    tm = min(ROW_TILE, M)
    period = cos.shape[0] // tm
    row = lambda n: pl.BlockSpec((tm, n), lambda i: (i, 0))
    full = lambda a: pl.BlockSpec(a.shape, lambda i: (0,) * a.ndim)
    rot = pl.BlockSpec((tm, 128), lambda i: (i % period, 0))
    outs = ((512, BF16), (256, F32), (256, F32), (128, F32), (64, F32), (256, BF16), (256, BF16), (256, BF16),
            (1024, BF16))
    return pl.pallas_call(
        functools.partial(_ab_proj_kernel, q_mult=q_mult),
        grid=(M // tm,),
        in_specs=[row(D_MODEL), full(lng), full(w_in), full(g_cq), full(w_uq), full(w_ukt), full(g_ckv), rot, rot],
        out_specs=[row(n) for n, _ in outs],
        out_shape=[jax.ShapeDtypeStruct((M, n), dt) for n, dt in outs],
        compiler_params=_cparams(1),
        name="ab_proj",
    )(h, lng, w_in, g_cq, w_uq, w_ukt, g_ckv, cos, sin)


def _diff_lambda(lq1_ref, lk1_ref, lq2_ref, lk2_ref, lam_init):
    a = jnp.sum(lq1_ref[...] * lk1_ref[...], axis=-1, keepdims=True)
    b = jnp.sum(lq2_ref[...] * lk2_ref[...], axis=-1, keepdims=True)
    return jnp.exp(a) - jnp.exp(b) + lam_init


def _softmax_steps(scores, vt, m_ref, l_ref, acc_ref):
    ps, alphas = [], []
    for i, s in enumerate(scores):
        m_prev = m_ref[i]
        m_new = jnp.maximum(m_prev, jnp.max(s, axis=0, keepdims=True))
        alpha = jnp.exp2(m_prev - m_new)
        p = jnp.exp2(s - m_new)
        l_ref[i] = alpha * l_ref[i] + jnp.sum(p, axis=0, keepdims=True)
        m_ref[i] = m_new
        ps.append(p.astype(BF16))
        alphas.append(alpha)
    pvs = [_dot(vt, p) for p in ps]
    for i, pv in enumerate(pvs):
        acc_ref[i] = alphas[i] * acc_ref[i] + pv


def _flash_a_kernel(qt_ref, k_ref, vt_ref, bt_ref, lq1_ref, lk1_ref, lq2_ref, lk2_ref, gh_ref,
                    o_ref, m_ref, l_ref, acc_ref, *, T, lam_init):
    qi = pl.program_id(2)
    m_ref[...] = jnp.full(m_ref.shape, NEG_INF, F32)
    l_ref[...] = jnp.zeros(l_ref.shape, F32)
    acc_ref[...] = jnp.zeros(acc_ref.shape, F32)

    def update(kb, bias_of_head):
        start = pl.multiple_of(kb * T, T)
        k = k_ref[pl.ds(start, T), :]
        vt = vt_ref[:, pl.ds(start, T)]
        scores = [_dot(k, qt_ref[c]) for c in range(2 * G_A)]
        if bias_of_head is not None:
            scores = [s + bias_of_head(c // 2) for c, s in enumerate(scores)]
        _softmax_steps(scores, vt, m_ref, l_ref, acc_ref)

    def far(kb, carry):
        update(kb, None)
        return carry

    lax.fori_loop(0, jnp.maximum(qi - 1, 0), far, 0)

    @pl.when(qi >= 1)
    def _():
        update(qi - 1, lambda hl: bt_ref[hl, 1])

    update(qi, lambda hl: bt_ref[hl, 0])

    lam = _diff_lambda(lq1_ref, lk1_ref, lq2_ref, lk2_ref, lam_init)
    for hl in range(G_A):
        o = acc_ref[2 * hl] / l_ref[2 * hl] - lam * (acc_ref[2 * hl + 1] / l_ref[2 * hl + 1])
        o = o * lax.rsqrt(jnp.mean(o * o, axis=0, keepdims=True) + EPS) * gh_ref[...] * (1.0 - lam_init)
        o_ref[:, hl * 128:(hl + 1) * 128] = o.T.astype(o_ref.dtype)


def _flash_a(qa, ka_bf, va_bf, tiles_a, lq1, lk1, lq2, lk2, g_head, batch, seq, lam_init):
    T = ATT_TILE
    nq = seq // T
    q5 = jnp.transpose(qa.reshape(batch, seq, KV_A, G_A, 2 * DK_A), (0, 2, 3, 4, 1))
    first = (jnp.arange(2 * DK_A) < DK_A)[:, None]
    zero = jnp.zeros((), qa.dtype)
    qt = jnp.stack([jnp.where(first, q5, zero), jnp.where(first, zero, q5)], axis=3)
    qt = qt.reshape(batch, KV_A, 2 * G_A, 2 * DK_A, seq)
    vt = jnp.transpose(va_bf.reshape(batch, seq, KV_A, DV_A), (0, 2, 3, 1))
    vec = pl.BlockSpec((1, DK_A), lambda b, g, i: (0, 0))
    return pl.pallas_call(
        functools.partial(_flash_a_kernel, T=T, lam_init=lam_init),
        grid=(batch, KV_A, nq),
        in_specs=[
            pl.BlockSpec((None, None, 2 * G_A, 2 * DK_A, T), lambda b, g, i: (b, g, 0, 0, i)),
            pl.BlockSpec((seq, 128), lambda b, g, i: (b, g)),
            pl.BlockSpec((None, None, DV_A, seq), lambda b, g, i: (b, g, 0, 0)),
            pl.BlockSpec((G_A, 2, T, T), lambda b, g, i: (g, 0, 0, 0)),
            vec, vec, vec, vec,
            pl.BlockSpec((DV_A, 1), lambda b, g, i: (0, 0)),
        ],
        out_specs=pl.BlockSpec((T, 256), lambda b, g, i: (b * nq + i, g)),
        out_shape=jax.ShapeDtypeStruct((batch * seq, H_A * DV_A), BF16),
        scratch_shapes=[
            pltpu.VMEM((2 * G_A, 1, T), F32),
            pltpu.VMEM((2 * G_A, 1, T), F32),
            pltpu.VMEM((2 * G_A, DV_A, T), F32),
        ],
        compiler_params=_cparams(3),
        name="flash_a",
    )(qt, ka_bf, vt, tiles_a, lq1, lk1, lq2, lk2, g_head.reshape(DV_A, 1))


def _flash_b_kernel(qt_ref, kc_ref, ct_ref, wuv_ref, o_ref, m_ref, l_ref, acc_ref, *, T):
    qi = pl.program_id(1)
    m_ref[...] = jnp.full(m_ref.shape, NEG_INF, F32)
    l_ref[...] = jnp.zeros(l_ref.shape, F32)
    acc_ref[...] = jnp.zeros(acc_ref.shape, F32)
    key = lax.broadcasted_iota(jnp.int32, (T, T), 0)
    qry = lax.broadcasted_iota(jnp.int32, (T, T), 1)

    def update(kb, diagonal):
        start = pl.multiple_of(kb * T, T)
        kc = kc_ref[pl.ds(start, T), :]
        ct = ct_ref[:, pl.ds(start, T)]
        scores = [_dot(kc, qt_ref[h]) for h in range(H_B)]
        if diagonal:
            scores = [jnp.where(qry >= key, s, NEG_INF) for s in scores]
        _softmax_steps(scores, ct, m_ref, l_ref, acc_ref)

    def far(kb, carry):
        update(kb, False)
        return carry

    lax.fori_loop(0, qi, far, 0)
    update(qi, True)
    for h in range(H_B):
        o_lat = (acc_ref[h] / l_ref[h]).T.astype(BF16)
        o_ref[:, h * 128:(h + 1) * 128] = _dot(o_lat, wuv_ref[h]).astype(o_ref.dtype)


def _flash_b(qb, kcat, w_uvt, batch, seq):
    T = ATT_TILE
    nq = seq // T
    qt = jnp.transpose(qb.reshape(batch, seq, H_B, 256), (0, 2, 3, 1))
    ct = jnp.transpose(kcat[:, 0:KV_RANK].reshape(batch, seq, KV_RANK), (0, 2, 1))
    return pl.pallas_call(
        functools.partial(_flash_b_kernel, T=T),
        grid=(batch, nq),
        in_specs=[
            pl.BlockSpec((None, H_B, 256, T), lambda b, i: (b, 0, 0, i)),
            pl.BlockSpec((seq, 256), lambda b, i: (b, 0)),
            pl.BlockSpec((None, KV_RANK, seq), lambda b, i: (b, 0, 0)),
            pl.BlockSpec((H_B, KV_RANK, DV_B), lambda b, i: (0, 0, 0)),
        ],
        out_specs=pl.BlockSpec((T, H_B * DV_B), lambda b, i: (b * nq + i, 0)),
        out_shape=jax.ShapeDtypeStruct((batch * seq, H_B * DV_B), BF16),
        scratch_shapes=[
            pltpu.VMEM((H_B, 1, T), F32),
            pltpu.VMEM((H_B, 1, T), F32),
            pltpu.VMEM((H_B, KV_RANK, T), F32),
        ],
        compiler_params=_cparams(2),
        name="flash_b",
    )(qt, kcat, ct, w_uvt)


def _sample_ab_kernel(pt_ref, qa_ref, qb_ref, knew_ref, vnew_ref, kcn_ref, ba_ref, lq1_ref, lk1_ref,
                      lq2_ref, lk2_ref, gh_ref, wuv_ref, *rest, P, dec, lam_init):
    kp = rest[0:P]
    vp = rest[P:2 * P]
    cp = rest[2 * P:3 * P]
    rp = rest[3 * P:4 * P]
    oa_ref, ob_ref, ma_ref, la_ref, acca_ref, mb_ref, lb_ref, accb_ref = rest[4 * P:]
    j = pl.program_id(1)
    last = j == pl.num_programs(1) - 1
    R = 2 * dec

    @pl.when(j == 0)
    def _():
        ma_ref[...] = jnp.full(ma_ref.shape, NEG_INF, F32)
        la_ref[...] = jnp.zeros(la_ref.shape, F32)
        acca_ref[...] = jnp.zeros(acca_ref.shape, F32)
        mb_ref[...] = jnp.full(mb_ref.shape, NEG_INF, F32)
        lb_ref[...] = jnp.zeros(lb_ref.shape, F32)
        accb_ref[...] = jnp.zeros(accb_ref.shape, F32)

    def online(s_list, v_list, m_ref, l_ref, acc_ref, idx):
        s = jnp.concatenate(s_list, axis=1)
        m_prev = m_ref[idx]
        m_new = jnp.maximum(m_prev, jnp.max(s, axis=-1, keepdims=True))
        alpha = jnp.exp(m_prev - m_new)
        p = jnp.exp(s - m_new)
        pv = _dot(p[:, 0:PAGE_SIZE].astype(BF16), v_list[0])
        for n in range(1, len(v_list)):
            pv = pv + _dot(p[:, n * PAGE_SIZE:(n + 1) * PAGE_SIZE].astype(BF16), v_list[n])
        l_ref[idx] = alpha * l_ref[idx] + jnp.sum(p, axis=-1, keepdims=True)
        acc_ref[idx] = alpha * acc_ref[idx] + pv
        m_ref[idx] = m_new

    for g in range(KV_A):
        v_list = [vp[n][pl.ds(g, PAGE_SIZE, stride=KV_A), :].astype(BF16) for n in range(P)]
        far = ba_ref[0, g]
        near = jnp.where(last, ba_ref[1, g], far)
        s_list = []
        for n in range(P):
            s = jnp.concatenate([_dot(qa_ref[0, g, mp], kp[n][g, mp].astype(BF16)) for mp in range(2)], axis=0)
            s_list.append(s + (near if n == P - 1 else far))
        online(s_list, v_list, ma_ref, la_ref, acca_ref, g)

    qb = qb_ref[0]
    c_list = [cp[n][...].astype(BF16) for n in range(P)]
    s_list = [_dot_nt(qb[:, 0:KV_RANK], c_list[n])
              + _dot(qb[:, KV_RANK:KV_RANK + DR_B], rp[n][...].astype(BF16)) for n in range(P)]
    online(s_list, c_list, mb_ref, lb_ref, accb_ref, 0)

    @pl.when(last)
    def _():
        lane_a = lax.broadcasted_iota(jnp.int32, (2 * R, PAGE_SIZE), 1)
        row_a = lax.broadcasted_iota(jnp.int32, (2 * R, DK_A), 0)

        def new_keys(qf, key_row, vals, bias, m_ref, l_ref, acc_ref, idx, lane):
            s = bias
            for t in range(dec):
                col = jnp.sum(qf * key_row(t), axis=-1, keepdims=True)
                s = s + jnp.where(lane == t, col, 0.0)
            m_prev = m_ref[idx]
            m_new = jnp.maximum(m_prev, jnp.max(s, axis=-1, keepdims=True))
            alpha = jnp.exp(m_prev - m_new)
            p = jnp.exp(s - m_new)
            pv = p[:, 0:1] * vals[0:1, :]
            for t in range(1, dec):
                pv = pv + p[:, t:t + 1] * vals[t:t + 1, :]
            l_ref[idx] = alpha * l_ref[idx] + jnp.sum(p, axis=-1, keepdims=True)
            acc_ref[idx] = alpha * acc_ref[idx] + pv
            m_ref[idx] = m_new

        lam = _diff_lambda(lq1_ref, lk1_ref, lq2_ref, lk2_ref, lam_init)
        for g in range(KV_A):
            qf = jnp.concatenate([qa_ref[0, g, 0], qa_ref[0, g, 1]], axis=0).astype(F32)
            key_row = lambda t, g=g: jnp.where(row_a < R, knew_ref[0, g, 0, t:t + 1, :], knew_ref[0, g, 1, t:t + 1, :])
            new_keys(qf, key_row, vnew_ref[0, g], ba_ref[2, g], ma_ref, la_ref, acca_ref, g, lane_a)
            on = acca_ref[g] / la_ref[g]
            o = on[0:R] - lam * on[R:2 * R]
            oa_ref[0, g] = (_rms(o, gh_ref[...]) * (1.0 - lam_init)).astype(oa_ref.dtype)

        RB = H_B * dec
        row = lax.broadcasted_iota(jnp.int32, (RB, PAGE_SIZE), 0)
        lane_b = lax.broadcasted_iota(jnp.int32, (RB, PAGE_SIZE), 1)
        bias_b = jnp.where((lane_b <= row % dec) & (lane_b < dec), 0.0, NEG_INF)
        kcn = kcn_ref[0]
        new_keys(qb.astype(F32), lambda t: kcn[t:t + 1, :], kcn[:, 0:KV_RANK], bias_b, mb_ref, lb_ref, accb_ref, 0,
                 lane_b)
        o_lat = (accb_ref[0] / lb_ref[0]).astype(BF16)
        out = jnp.zeros((RB, DV_B), F32)
        for h in range(H_B):
            out = out + jnp.where(row // dec == h, _dot(o_lat, wuv_ref[h]), 0.0)
        ob_ref[0] = out.astype(ob_ref.dtype)


def _sample_ab(layer, page_table, qa, qb, knew, vnew, kcn, bias_a, lq1, lk1, lq2, lk2, g_head, w_uvt,
               cache_k, cache_v, cache_c, cache_r, dec, lam_init):
    n_seq, n_pages = page_table.shape
    P = PAGES_PER_STEP
    R = 2 * dec
    const = lambda a: pl.BlockSpec(a.shape, lambda n, j, pt: (0,) * a.ndim)
    seq_blk = lambda a: pl.BlockSpec((1,) + a.shape[1:], lambda n, j, pt: (n,) + (0,) * (a.ndim - 1))

    def page_spec(a, p):
        nd = a.ndim - 2
        return pl.BlockSpec((None, None) + a.shape[2:],
                            lambda n, j, pt: (layer, pt[n, j * P + p]) + (0,) * nd)

    pages = ([page_spec(cache_k, p) for p in range(P)] + [page_spec(cache_v, p) for p in range(P)]
             + [page_spec(cache_c, p) for p in range(P)] + [page_spec(cache_r, p) for p in range(P)])
    args = (qa, qb, knew, vnew, kcn, bias_a, lq1, lk1, lq2, lk2, g_head, w_uvt)
    in_specs = [seq_blk(qa), seq_blk(qb), seq_blk(knew), seq_blk(vnew), seq_blk(kcn), const(bias_a),
                const(lq1), const(lk1), const(lq2), const(lk2), const(g_head), const(w_uvt)] + pages
    grid_spec = pltpu.PrefetchScalarGridSpec(
        num_scalar_prefetch=1,
        grid=(n_seq, n_pages // P),
        in_specs=in_specs,
        out_specs=[pl.BlockSpec((1, KV_A, R, DV_A), lambda n, j, pt: (n, 0, 0, 0)),
                   pl.BlockSpec((1, H_B * dec, DV_B), lambda n, j, pt: (n, 0, 0))],
        scratch_shapes=[
            pltpu.VMEM((KV_A, 2 * R, 1), F32), pltpu.VMEM((KV_A, 2 * R, 1), F32),
            pltpu.VMEM((KV_A, 2 * R, DV_A), F32),
            pltpu.VMEM((1, H_B * dec, 1), F32), pltpu.VMEM((1, H_B * dec, 1), F32),
            pltpu.VMEM((1, H_B * dec, KV_RANK), F32),
        ],
    )
    return pl.pallas_call(
        functools.partial(_sample_ab_kernel, P=P, dec=dec, lam_init=lam_init),
        grid_spec=grid_spec,
        out_shape=[jax.ShapeDtypeStruct((n_seq, KV_A, R, DV_A), BF16),
                   jax.ShapeDtypeStruct((n_seq, H_B * dec, DV_B), BF16)],
        compiler_params=_cparams(2),
        name="sample_ab",
    )(page_table, *args, *([cache_k] * P + [cache_v] * P + [cache_c] * P + [cache_r] * P))


def _out_proj_kernel(a_ref, b_ref, w_ref, res_ref, o_ref):
    ka = a_ref.shape[1]
    y = _dot(a_ref[...], w_ref[0:ka, :]) + _dot(b_ref[...], w_ref[ka:, :])
    o_ref[...] = res_ref[...] + y


def _out_proj(a, b, w, res):
    M = res.shape[0]
    tm = min(ROW_TILE, M)
    row = lambda n: pl.BlockSpec((tm, n), lambda i: (i, 0))
    return pl.pallas_call(
        _out_proj_kernel,
        grid=(M // tm,),
        in_specs=[row(a.shape[1]), row(b.shape[1]), pl.BlockSpec(w.shape, lambda i: (0, 0)), row(D_MODEL)],
        out_specs=row(D_MODEL),
        out_shape=jax.ShapeDtypeStruct((M, D_MODEL), F32),
        compiler_params=_cparams(1),
        name="out_proj",
    )(a, b, w, res)


def _c_proj_kernel(*refs, dils):
    h_ref, g_ref, w_ref, q_ref, k_ref, v_ref = refs[0:6]
    n = H_C * DH_C
    xn = _rms(h_ref[...], g_ref[...]).astype(BF16)
    z = _dot(xn, w_ref[...])
    q = z[:, 0:n] * (DH_C ** -0.5)
    q_ref[...] = q.astype(BF16)
    k_ref[...] = z[:, n:2 * n]
    v_ref[...] = z[:, 2 * n:3 * n]
    if not dils:
        return
    kbf_ref, vbf_ref = refs[6:8]
    cls_refs = refs[8:8 + 3 * len(dils)]
    zs_ref = refs[8 + 3 * len(dils)]
    kbf_ref[...] = z[:, n:2 * n].astype(BF16)
    vbf_ref[...] = z[:, 2 * n:3 * n].astype(BF16)
    for j in range(H_C):
        zs_ref[j] = q[:, j * DH_C:(j + 1) * DH_C]
    for j in range(H_C, 3 * H_C):
        zs_ref[j] = z[:, j * DH_C:(j + 1) * DH_C]
    tm = zs_ref.shape[1]
    for di, dil in enumerate(dils):
        for part in range(3):
            o_ref = cls_refs[3 * di + part]
            for r in range(dil):
                for j in range(H_C):
                    rows = zs_ref[part * H_C + j, pl.ds(r, tm // dil, stride=dil), :]
                    o_ref[r, :, j * DH_C:(j + 1) * DH_C] = rows.astype(BF16)


def _c_proj(h, g, w, batch=None, seq=None, dils=()):
    M = h.shape[0]
    tm = min(ROW_TILE, M)
    n = H_C * DH_C
    row = lambda c: pl.BlockSpec((tm, c), lambda i: (i, 0))
    out_specs = [row(n)] * 3
    out_shape = [jax.ShapeDtypeStruct((M, n), dt) for dt in (BF16, F32, F32)]
    scratch = []
    if dils:
        tiles = seq // tm
        out_specs += [row(n)] * 2
        out_shape += [jax.ShapeDtypeStruct((M, n), BF16)] * 2
        for dil in dils:
            out_specs += [pl.BlockSpec((None, dil, tm // dil, n), lambda i: (i // tiles, 0, i % tiles, 0))] * 3
            out_shape += [jax.ShapeDtypeStruct((batch, dil, seq // dil, n), BF16)] * 3
        scratch = [pltpu.VMEM((3 * H_C, tm, DH_C), F32)]
    return pl.pallas_call(
        functools.partial(_c_proj_kernel, dils=tuple(dils)),
        grid=(M // tm,),
        in_specs=[row(D_MODEL), pl.BlockSpec(g.shape, lambda i: (0, 0)), pl.BlockSpec(w.shape, lambda i: (0, 0))],
        out_specs=out_specs,
        out_shape=out_shape,
        scratch_shapes=scratch,
        compiler_params=_cparams(1),
        name="c_proj_classes" if dils else "c_proj",
    )(h, g, w)


def _dilated_kernel(q_ref, kp_ref, kc_ref, vp_ref, vc_ref, bt_ref, o_ref, lse_ref):
    has_prev = pl.program_id(2) > 0
    heads = [slice(h * DH_C, (h + 1) * DH_C) for h in range(H_C)]
    scores = [(_dot_nt(q_ref[:, sl], kp_ref[:, sl]), _dot_nt(q_ref[:, sl], kc_ref[:, sl])) for sl in heads]
    probs = []
    for h, (sp, sc) in enumerate(scores):
        bt = bt_ref[h]
        sp = jnp.where(has_prev, sp + bt[:, 0:BAND], NEG_INF)
        sc = sc + bt[:, BAND:2 * BAND]
        m = jnp.maximum(jnp.max(sp, axis=-1, keepdims=True), jnp.max(sc, axis=-1, keepdims=True))
        pp = jnp.exp(sp - m)
        pc = jnp.exp(sc - m)
        l = jnp.sum(pp, axis=-1, keepdims=True) + jnp.sum(pc, axis=-1, keepdims=True)
        lse_ref[:, h * LSE_W:(h + 1) * LSE_W] = jnp.broadcast_to(m + jnp.log(l), (BAND, LSE_W))
        probs.append((pp.astype(BF16), pc.astype(BF16), l))
    outs = [_dot(pp, vp_ref[:, sl]) + _dot(pc, vc_ref[:, sl]) for (pp, pc, _), sl in zip(probs, heads)]
    for o, (_, _, l), sl in zip(outs, probs, heads):
        o_ref[:, sl] = (o / l).astype(o_ref.dtype)


def _dilated_branch(q, k, v, tiles_c, branch, batch, seq):
    window, dil = DILATED[branch]
    assert window // dil == BAND and seq % (dil * BAND) == 0
    L = seq // dil
    nb = L // BAND
    n = H_C * DH_C
    cur = pl.BlockSpec((None, None, BAND, n), lambda b, r, i: (b, r, i, 0))
    prev = pl.BlockSpec((None, None, BAND, n), lambda b, r, i: (b, r, jnp.maximum(i - 1, 0), 0))
    return pl.pallas_call(
        _dilated_kernel,
        grid=(batch, dil, nb),
        in_specs=[cur, prev, cur, prev, cur,
                  pl.BlockSpec((None, H_C, BAND, 2 * BAND), lambda b, r, i: (branch, 0, 0, 0))],
        out_specs=[cur, pl.BlockSpec((None, None, BAND, H_C * LSE_W), lambda b, r, i: (b, r, i, 0))],
        out_shape=[jax.ShapeDtypeStruct((batch, dil, L, n), F32),
                   jax.ShapeDtypeStruct((batch, dil, L, H_C * LSE_W), F32)],
        compiler_params=_cparams(3),
        name=f"dilated_{dil}",
    )(q, k, k, v, v, tiles_c)


def _c_out_kernel(*refs, dils):
    nb = len(dils)
    o_refs = list(refs[0:nb])
    l_refs = list(refs[nb:2 * nb])
    w_ref, res_ref, out_ref, mix_ref = refs[2 * nb:2 * nb + 4]
    nat = refs[2 * nb + 4:]
    tm = mix_ref.shape[0]
    pos = 0
    for b, dil in enumerate(dils):
        if dil == 1:
            continue
        on_ref, ln_ref = nat[pos:pos + 2]
        pos += 2
        for r in range(dil):
            rows = pl.ds(r, tm // dil, stride=dil)
            for h in range(H_C):
                on_ref[h, rows, :] = o_refs[b][r, :, h * DH_C:(h + 1) * DH_C]
            ln_ref[rows, :] = l_refs[b][r]
        o_refs[b], l_refs[b] = on_ref, ln_ref
    ls = tuple(l[...] for l in l_refs)
    m = jnp.maximum(jnp.maximum(ls[0], ls[1]), ls[2])
    e = [jnp.exp(l - m) for l in ls]
    den = e[0] + e[1] + e[2]
    wts = [x / den for x in e]
    for h in range(H_C):
        sl = slice(h * DH_C, (h + 1) * DH_C)
        mix = None
        for b, dil in enumerate(dils):
            o = o_refs[b][:, sl] if dil == 1 else o_refs[b][h]
            term = wts[b][:, h * LSE_W:h * LSE_W + 1] * o
            mix = term if mix is None else mix + term
        mix_ref[:, sl] = mix.astype(BF16)
    out_ref[...] = res_ref[...] + _dot(mix_ref[...], w_ref[...])


def _c_out(outs, lses, w, res, seq):
    M = res.shape[0]
    tm = 256
    n = H_C * DH_C
    tiles = seq // tm
    dils = tuple(dil for _, dil in DILATED)
    row = lambda c: pl.BlockSpec((tm, c), lambda i: (i, 0))

    def cls(c, dil):
        if dil == 1:
            return pl.BlockSpec((None, None, tm, c), lambda i: (i // tiles, 0, i % tiles, 0))
        return pl.BlockSpec((None, dil, tm // dil, c), lambda i: (i // tiles, 0, i % tiles, 0))

    scratch = [pltpu.VMEM((tm, n), BF16)]
    for dil in dils:
        if dil > 1:
            scratch += [pltpu.VMEM((H_C, tm, DH_C), F32), pltpu.VMEM((tm, H_C * LSE_W), F32)]
    return pl.pallas_call(
        functools.partial(_c_out_kernel, dils=dils),
        grid=(M // tm,),
        in_specs=([cls(n, dil) for dil in dils] + [cls(H_C * LSE_W, dil) for dil in dils]
                  + [pl.BlockSpec(w.shape, lambda i: (0, 0)), row(D_MODEL)]),
        out_specs=row(D_MODEL),
        out_shape=jax.ShapeDtypeStruct((M, D_MODEL), F32),
        scratch_shapes=scratch,
        compiler_params=_cparams(1),
        name="c_out",
    )(*outs, *lses, w, res)


def _sample_c_kernel(q_ref, knew_ref, vnew_ref, kst_ref, vst_ref, knx_ref, vnx_ref, bm_ref, cnt_ref,
                     bmn_ref, cntn_ref, prev_k_ref, prev_v_ref, o_ref, ok_ref, ov_ref, m_ref, l_ref, acc_ref,
                     *, dec, chunk):
    del prev_k_ref, prev_v_ref
    c = pl.program_id(1)
    last = c == pl.num_programs(1) - 1

    @pl.when(c == 0)
    def _():
        m_ref[...] = jnp.full(m_ref.shape, NEG_INF, F32)
        l_ref[...] = jnp.zeros(l_ref.shape, F32)
        acc_ref[...] = jnp.zeros(acc_ref.shape, F32)

    q = q_ref[0]
    kf = kst_ref[0, 0].reshape(chunk * H_C, DH_C).astype(BF16)
    vf = vst_ref[0, 0].reshape(chunk * H_C, DH_C).astype(BF16)
    s = _dot_nt(q, kf) + bm_ref[c]
    m_prev = m_ref[...]
    m_new = jnp.maximum(m_prev, jnp.max(s, axis=-1, keepdims=True))
    m_use = jnp.where(m_new == NEG_INF, 0.0, m_new)
    alpha = jnp.exp(m_prev - m_use)
    w = cnt_ref[c] * jnp.exp(s - m_use)
    l_ref[...] = alpha * l_ref[...] + jnp.sum(w, axis=-1, keepdims=True)
    acc_ref[...] = alpha * acc_ref[...] + _dot(w.astype(BF16), vf)
    m_ref[...] = m_new

    ok_ref[0, 0, 0:chunk - dec] = kst_ref[0, 0, dec:chunk]
    ov_ref[0, 0, 0:chunk - dec] = vst_ref[0, 0, dec:chunk]

    @pl.when(jnp.logical_not(last))
    def _():
        ok_ref[0, 0, chunk - dec:chunk] = knx_ref[0, 0]
        ov_ref[0, 0, chunk - dec:chunk] = vnx_ref[0, 0]

    @pl.when(last)
    def _():
        ok_ref[0, 0, chunk - dec:chunk] = knew_ref[0]
        ov_ref[0, 0, chunk - dec:chunk] = vnew_ref[0]
        nk = dec * H_C
        knf = knew_ref[0].reshape(nk, DH_C).astype(BF16)
        vnf = vnew_ref[0].reshape(nk, DH_C).astype(BF16)
        sn = _dot_nt(q, knf) + bmn_ref[...]
        m_fin = jnp.maximum(m_new, jnp.max(sn, axis=-1, keepdims=True))
        a_fin = jnp.exp(m_new - m_fin)
        wn = cntn_ref[...] * jnp.exp(sn - m_fin)
        l_fin = a_fin * l_ref[...] + jnp.sum(wn, axis=-1, keepdims=True)
        o_ref[0] = ((a_fin * acc_ref[...] + _dot(wn.astype(BF16), vnf)) / l_fin).astype(o_ref.dtype)


def _sample_c(layer, q, knew, vnew, state_k, state_v, prev_k, prev_v, bm, cnt, bmn, cntn, dec):
    n_layers, n_seq, win_rows = state_k.shape[:3]
    chunk = WIN_CHUNK
    nch = win_rows // chunk
    n = H_C * DH_C
    seq_blk = lambda a: pl.BlockSpec((1,) + a.shape[1:], lambda s, c: (s,) + (0,) * (a.ndim - 1))
    const = lambda a: pl.BlockSpec(a.shape, lambda s, c: (0,) * a.ndim)
    st = pl.BlockSpec((1, 1, chunk, H_C, DH_C), lambda s, c: (layer, s, c, 0, 0))
    nxt = pl.BlockSpec((1, 1, dec, H_C, DH_C),
                       lambda s, c: (layer, s, jnp.minimum((c + 1) * (chunk // dec), win_rows // dec - 1), 0, 0))
    in_specs = [seq_blk(q), seq_blk(knew), seq_blk(vnew), st, st, nxt, nxt,
                const(bm), const(cnt), const(bmn), const(cntn)]
    args = [q, knew, vnew, state_k, state_v, state_k, state_v, bm, cnt, bmn, cntn]
    aliases = {}
    if prev_k is not None:
        in_specs += [pl.BlockSpec(memory_space=pl.ANY)] * 2
        aliases = {len(args): 1, len(args) + 1: 2}
        args += [prev_k, prev_v]
        kern = _sample_c_kernel
    else:
        kern = lambda *refs, **kw: _sample_c_kernel(*refs[:11], None, None, *refs[11:], **kw)
    win_shape = jax.ShapeDtypeStruct((n_layers, n_seq, win_rows, H_C, DH_C), F32)
    return pl.pallas_call(
        functools.partial(kern, dec=dec, chunk=chunk),
        grid=(n_seq, nch),
        in_specs=in_specs,
        out_specs=[pl.BlockSpec((1, H_C * dec, DH_C), lambda s, c: (s, 0, 0)), st, st],
        out_shape=[jax.ShapeDtypeStruct((n_seq, H_C * dec, DH_C), BF16), win_shape, win_shape],
        scratch_shapes=[pltpu.VMEM((H_C * dec, 1), F32), pltpu.VMEM((H_C * dec, 1), F32),
                        pltpu.VMEM((H_C * dec, DH_C), F32)],
        input_output_aliases=aliases,
        compiler_params=_cparams(2),
        name="sample_c",
    )(*args)


def _ffn_kernel(*refs, tm, nf, seq_len, final_norm):
    short = seq_len < tm
    h_ref, g_ref, wg_ref, wu_ref, cw_ref, cb_ref, wd_ref = refs[0:7]
    pos = 7
    if short:
        s1_ref, s2_ref = refs[pos:pos + 2]
        pos += 2
    if final_norm:
        gf_ref = refs[pos]
        pos += 1
    out_ref, cst_ref, xn_ref, acc_ref, ext_ref = refs[pos:pos + 5]
    carry_ref = None if short else refs[pos + 5]
    i = pl.program_id(0)
    j = pl.program_id(1)

    @pl.when(j == 0)
    def _():
        xn_ref[...] = _rms(h_ref[...], g_ref[...]).astype(BF16)
        acc_ref[...] = jnp.zeros(acc_ref.shape, F32)

    xn = xn_ref[...]
    tf = wg_ref.shape[1]
    if short:
        ext_ref[0:8] = jnp.zeros((8, tf), F32)
        t = lax.broadcasted_iota(jnp.int32, (tm, 1), 0) % seq_len
    else:
        first = i % (seq_len // tm) == 0

        @pl.when(first)
        def _():
            ext_ref[0:8] = jnp.zeros((8, tf), F32)

        @pl.when(jnp.logical_not(first))
        def _():
            ext_ref[0:8] = carry_ref[j]

    down = None
    for c0 in range(0, tf, FF_GROUP):
        cols = slice(c0, min(c0 + FF_GROUP, tf))
        gate = _dot(xn, wg_ref[:, cols])
        up = _dot(xn, wu_ref[:, cols])
        ext_ref[8:8 + tm, cols] = gate
        if short:
            cst_ref[:, cols] = gate
        else:
            carry_ref[j, :, cols] = gate[tm - 8:tm]
            cst_ref[:, cols] = gate[tm - 8:tm]
        g1 = ext_ref[7:7 + tm, cols]
        g2 = ext_ref[6:6 + tm, cols]
        if short:
            g1 = jnp.where(t >= 1, g1, s1_ref[:, cols])
            g2 = jnp.where(t >= 2, g2, s2_ref[:, cols])
        conv = cb_ref[:, cols] + g2 * cw_ref[0:1, cols] + g1 * cw_ref[1:2, cols] + gate * cw_ref[2:3, cols]
        y = conv * jax.nn.sigmoid(conv) * up
        d = _dot(y.astype(BF16), wd_ref[cols, :])
        down = d if down is None else down + d
    acc_ref[...] += down

    @pl.when(j == nf - 1)
    def _():
        out = h_ref[...] + acc_ref[...]
        if final_norm:
            out = _rms(out, gf_ref[...])
        out_ref[...] = out


def _ffn(h, g, w_up, conv_w, conv_b, w_down, seq_len, shifted=None, final_gain=None):
    M = h.shape[0]
    tm = min(ROW_TILE, M)
    tf = FF_TILE
    nf = D_FF // tf
    short = seq_len < tm
    row = pl.BlockSpec((tm, D_MODEL), lambda i, j: (i, 0))
    in_specs = [row, pl.BlockSpec((1, D_MODEL), lambda i, j: (0, 0)),
                pl.BlockSpec((D_MODEL, tf), lambda i, j: (0, j)),
                pl.BlockSpec((D_MODEL, tf), lambda i, j: (0, nf + j)),
                pl.BlockSpec((CONV_W, tf), lambda i, j: (0, j)),
                pl.BlockSpec((1, tf), lambda i, j: (0, j)),
                pl.BlockSpec((tf, D_MODEL), lambda i, j: (j, 0))]
    args = [h, g, w_up, w_up, conv_w, conv_b, w_down]
    scratch = [pltpu.VMEM((tm, D_MODEL), BF16), pltpu.VMEM((tm, D_MODEL), F32), pltpu.VMEM((tm + 8, tf), F32)]
    if short:
        in_specs += [pl.BlockSpec((tm, tf), lambda i, j: (i, j))] * 2
        args += list(shifted)
        cst_spec = pl.BlockSpec((tm, tf), lambda i, j: (i, j))
        cst_shape = jax.ShapeDtypeStruct((M, D_FF), F32)
    else:
        cst_spec = pl.BlockSpec((None, 8, tf), lambda i, j: (i, 0, j))
        cst_shape = jax.ShapeDtypeStruct((M // tm, 8, D_FF), F32)
        scratch.append(pltpu.VMEM((nf, 8, tf), F32))
    if final_gain is not None:
        in_specs.append(pl.BlockSpec((1, D_MODEL), lambda i, j: (0, 0)))
        args.append(final_gain)
    return pl.pallas_call(
        functools.partial(_ffn_kernel, tm=tm, nf=nf, seq_len=seq_len, final_norm=final_gain is not None),
        grid=(M // tm, nf),
        in_specs=in_specs,
        out_specs=[row, cst_spec],
        out_shape=[jax.ShapeDtypeStruct((M, D_MODEL), F32), cst_shape],
        scratch_shapes=scratch,
        compiler_params=_cparams(2),
        name="ffn_short" if short else "ffn",
    )(*args)


def _rope_tables(pos):
    half = DR_B // 2
    inv = ROPE_BASE ** (-jnp.arange(half, dtype=F32) / half)
    ang = pos.astype(F32)[:, None] * inv[None, :]
    cos, sin = jnp.cos(ang), jnp.sin(ang)
    pad = jnp.zeros((pos.shape[0], 128 - DR_B), F32)
    return jnp.concatenate([cos, cos, pad], axis=1), jnp.concatenate([-sin, sin, pad], axis=1)


def _swap_halves(w):
    half = w.shape[-1] // 2
    return jnp.concatenate([w[..., half:], w[..., :half]], axis=-1)


def _prep_ab_weights(w_in, w_uq, w_uk, w_uv):
    kr = w_in[:, 1408:1472]
    z64 = jnp.zeros((D_MODEL, 64), F32)
    w_in_aug = jnp.concatenate([w_in[:, :1408], kr, z64, _swap_halves(kr), z64], axis=1).astype(BF16)
    nope = w_uq[:, :, :DN_B]
    rope = w_uq[:, :, DN_B:]
    zq = jnp.zeros((Q_RANK, H_B, 64), F32)
    main = jnp.concatenate([nope, rope, zq], axis=-1).reshape(Q_RANK, H_B * 256)
    swapped = jnp.concatenate([_swap_halves(rope), zq], axis=-1).reshape(Q_RANK, H_B * 128)
    w_uq_aug = jnp.concatenate([main, swapped], axis=1).astype(BF16)
    w_ukt = jnp.transpose(w_uk, (1, 2, 0)).astype(BF16)
    w_uvt = jnp.transpose(w_uv, (1, 0, 2)).astype(BF16)
    return w_in_aug, w_uq_aug, w_ukt, w_uvt


def kernel(x_prompt, x_sample, cache_a_k, cache_a_v, cache_mla_ckv, cache_mla_krope, state_win_k, state_win_v,
           state_conv, page_table, ln_mix, ln_ffn, ln_final, rel_bias, w_in_ab, lam_q1, lam_k1, lam_q2, lam_k2,
           g_head_a, g_cq, g_ckv, w_uq, w_uk, w_uv, w_out_ab, w_in_c, w_out_c, w_up, conv_w, conv_b, w_down):
    batch, seq, _ = x_prompt.shape
    n_seq, dec, _ = x_sample.shape
    n_pages = page_table.shape[1]
    past = n_pages * PAGE_SIZE
    depth = ln_mix.shape[0]
    win_rows = state_win_k.shape[2]
    n_pool = cache_a_k.shape[1]
    assert past == cache_a_k.shape[2] * n_pages and win_rows == WIN_MAX and past >= WIN_MAX
    assert seq >= WIN_MAX and seq % ROW_TILE == 0 and (n_seq * dec) % 8 == 0

    tiles_a, tiles_c, bias_sa, sc_bm, sc_cnt, sc_bmn, sc_cntn = _bias_tiles(rel_bias, past, dec, win_rows)
    cos_p, sin_p = _rope_tables(jnp.arange(seq, dtype=jnp.int32))
    cos_s, sin_s = _rope_tables(past + jnp.arange(dec, dtype=jnp.int32))
    cos_s, sin_s = jnp.tile(cos_s, (n_seq, 1)), jnp.tile(sin_s, (n_seq, 1))

    cache_kt = jnp.transpose(cache_a_k, (0, 1, 3, 4, 5, 2))
    cache_rt = jnp.transpose(cache_mla_krope, (0, 1, 3, 2))
    cache_v2 = cache_a_v.reshape(cache_a_v.shape[0], n_pool, PAGE_SIZE * KV_A, DV_A)
    hp = x_prompt.reshape(batch * seq, D_MODEL)
    hs = x_sample.reshape(n_seq * dec, D_MODEL)
    row2 = lambda a: a.reshape(1, -1)
    outs = {k: [] for k in ("ak_p", "ak_s", "av_p", "av_s", "ck_p", "ck_s", "kr_p", "kr_s", "wk_p", "wv_p",
                            "cv_p", "cv_s")}
    win_k = win_v = None

    for li in range(depth):
        i = li // 2
        lng = row2(ln_mix[li])
        if li % 2 == 0:
            lam_init = 0.8 - 0.6 * math.exp(-0.3 * li)
            w_in_aug, w_uq_aug, w_ukt, w_uvt = _prep_ab_weights(w_in_ab[i], w_uq[i], w_uk[i], w_uv[i])
            w_out = w_out_ab[i].astype(BF16)
            lam_vecs = (row2(lam_q1[i]), row2(lam_k1[i]), row2(lam_q2[i]), row2(lam_k2[i]))
            gh = row2(g_head_a[i])
            proj = lambda h, cos, sin, q_mult: _ab_proj(h, lng, w_in_aug, row2(g_cq[i]), w_uq_aug, w_ukt,
                                                        row2(g_ckv[i]), cos, sin, q_mult)
            qa, ka, va, ckv, kr, ka_bf, va_bf, kcat, qb = proj(hp, cos_p, sin_p, LOG2E)
            o_a = _flash_a(qa, ka_bf, va_bf, tiles_a, *lam_vecs, gh, batch, seq, lam_init)
            o_b = _flash_b(qb, kcat, w_uvt, batch, seq)
            hp = _out_proj(o_a, o_b, w_out, hp)
            outs["ak_p"].append(ka.reshape(batch, seq, KV_A, 2, DK_A))
            outs["av_p"].append(va.reshape(batch, seq, KV_A, DV_A))
            outs["ck_p"].append(ckv.reshape(batch, seq, KV_RANK))
            outs["kr_p"].append(kr.reshape(batch, seq, DR_B))
            qa, ka, va, ckv, kr, _, _, _, qb = proj(hs, cos_s, sin_s, 1.0)
            qa_s = jnp.transpose(qa.reshape(n_seq, dec, KV_A, G_A, 2, DK_A), (0, 2, 4, 3, 1, 5))
            qa_s = qa_s.reshape(n_seq, KV_A, 2, G_A * dec, DK_A)
            qb_s = jnp.transpose(qb.reshape(n_seq, dec, H_B, 256), (0, 2, 1, 3)).reshape(n_seq, H_B * dec, 256)
            knew = jnp.transpose(ka.reshape(n_seq, dec, KV_A, 2, DK_A), (0, 2, 3, 1, 4))
            vnew = jnp.transpose(va.reshape(n_seq, dec, KV_A, DV_A), (0, 2, 1, 3))
            kcn = jnp.concatenate([ckv, kr, jnp.zeros((n_seq * dec, 256 - KV_RANK - DR_B), F32)], axis=1)
            o_a, o_b = _sample_ab(i, page_table, qa_s, qb_s, knew, vnew, kcn.reshape(n_seq, dec, 256), bias_sa,
                                  *lam_vecs, gh, w_uvt, cache_kt, cache_v2, cache_mla_ckv, cache_rt, dec, lam_init)
            o_a = jnp.transpose(o_a.reshape(n_seq, KV_A, G_A, dec, DV_A), (0, 3, 1, 2, 4))
            o_b = jnp.transpose(o_b.reshape(n_seq, H_B, dec, DV_B), (0, 2, 1, 3))
            hs = _out_proj(o_a.reshape(n_seq * dec, H_A * DV_A), o_b.reshape(n_seq * dec, H_B * DV_B), w_out, hs)
            outs["ak_s"].append(ka.reshape(n_seq, dec, KV_A, 2, DK_A))
            outs["av_s"].append(va.reshape(n_seq, dec, KV_A, DV_A))
            outs["ck_s"].append(ckv.reshape(n_seq, dec, KV_RANK))
            outs["kr_s"].append(kr.reshape(n_seq, dec, DR_B))
        else:
            w_in = w_in_c[i].astype(BF16)
            w_out = w_out_c[i].astype(BF16)
            dils = tuple(dil for _, dil in DILATED if dil > 1)
            q, k, v, k_bf, v_bf, *cls = _c_proj(hp, lng, w_in, batch, seq, dils)
            by_class = lambda a: a.reshape(batch, 1, seq, H_C * DH_C)
            qkv = {1: (by_class(q), by_class(k_bf), by_class(v_bf))}
            for di, dil in enumerate(dils):
                qkv[dil] = tuple(cls[3 * di:3 * di + 3])
            branches = [_dilated_branch(*qkv[dil], tiles_c, b, batch, seq) for b, (_, dil) in enumerate(DILATED)]
            hp = _c_out([o for o, _ in branches], [l for _, l in branches], w_out, hp, seq)
            keep = min(WIN_MAX, seq)
            outs["wk_p"].append(k.reshape(batch, seq, H_C, DH_C)[:, seq - keep:])
            outs["wv_p"].append(v.reshape(batch, seq, H_C, DH_C)[:, seq - keep:])
            q, k, v = _c_proj(hs, lng, w_in)
            q_s = jnp.transpose(q.reshape(n_seq, dec, H_C, DH_C), (0, 2, 1, 3)).reshape(n_seq, H_C * dec, DH_C)
            o, win_k, win_v = _sample_c(i, q_s, k.reshape(n_seq, dec, H_C, DH_C), v.reshape(n_seq, dec, H_C, DH_C),
                                        state_win_k, state_win_v, win_k, win_v, sc_bm, sc_cnt, sc_bmn, sc_cntn, dec)
            o = jnp.transpose(o.reshape(n_seq, H_C, dec, DH_C), (0, 2, 1, 3)).reshape(n_seq * dec, H_C * DH_C)
            half = (H_C * DH_C) // 2
            hs = _out_proj(o[:, :half], o[:, half:], w_out, hs)
        w_up_bf = w_up[li].astype(BF16)
        w_down_bf = w_down[li].astype(BF16)
        last = li == depth - 1
        fg = row2(ln_final) if last else None
        hp, cst = _ffn(hp, row2(ln_ffn[li]), w_up_bf, conv_w[li], row2(conv_b[li]), w_down_bf, seq, final_gain=fg)
        tiles = seq // ROW_TILE
        outs["cv_p"].append(cst[tiles - 1::tiles, 8 - (CONV_W - 1):])
        prev = state_conv[li]
        zrow = jnp.zeros((n_seq, 1, D_FF), F32)
        s1 = jnp.concatenate([prev[:, 1:2]] + [zrow] * (dec - 1), axis=1).reshape(n_seq * dec, D_FF)
        s2 = jnp.concatenate([prev] + [zrow] * (dec - 2), axis=1).reshape(n_seq * dec, D_FF)
        hs, gate = _ffn(hs, row2(ln_ffn[li]), w_up_bf, conv_w[li], row2(conv_b[li]), w_down_bf, dec,
                        shifted=(s1, s2), final_gain=fg)
        outs["cv_s"].append(gate.reshape(n_seq, dec, D_FF)[:, dec - (CONV_W - 1):])

    st = lambda name: jnp.stack(outs[name])
    return (hp.reshape(batch, seq, D_MODEL), hs.reshape(n_seq, dec, D_MODEL),
            st("ak_p"), st("ak_s"), st("av_p"), st("av_s"), st("ck_p"), st("ck_s"), st("kr_p"), st("kr_s"),
            st("wk_p"), win_k, st("wv_p"), win_v, st("cv_p"), st("cv_s"))
```

```python
import functools
import math

import jax
import jax.numpy as jnp
from jax import lax
from jax.experimental import pallas as pl
from jax.experimental.pallas import tpu as pltpu

F32 = jnp.float32
BF16 = jnp.bfloat16
NEG_INF = float("-inf")

D_MODEL = 1024
H_A, KV_A, G_A, DK_A, DV_A = 4, 2, 2, 64, 128
H_B, Q_RANK, KV_RANK, DN_B, DR_B, DV_B = 4, 256, 128, 128, 64, 128
H_C, DH_C = 8, 128
DILATED = ((128, 1), (512, 4), (2048, 16))
BAND = 128
LSE_W = 128 // H_C
WIN_MAX = 2048
NUM_BUCKETS, MAX_DISTANCE = 32, 128
D_FF, CONV_W = 2816, 3
ROPE_BASE = 10000.0
EPS = 1e-6
PAGE_SIZE = 128

ROW_TILE = 512
FF_TILE = 1408
FF_GROUP = 768
ATT_TILE = 512
LOG2E = math.log2(math.e)
PAGES_PER_STEP = 32
WIN_CHUNK = 1024
VMEM_LIMIT = 56 * 1024 * 1024


def _cparams(n_axes, vmem=VMEM_LIMIT):
    return pltpu.CompilerParams(dimension_semantics=("arbitrary",) * n_axes, vmem_limit_bytes=vmem)


def _dot(a, b):
    return jnp.dot(a, b, preferred_element_type=F32)


def _dot_nt(a, b):
    return lax.dot_general(a, b, (((1,), (1,)), ((), ())), preferred_element_type=F32)


def _rms(x, g):
    return x * lax.rsqrt(jnp.mean(x * x, axis=-1, keepdims=True) + EPS) * g


def _bucket_starts():
    half = NUM_BUCKETS // 2
    starts = list(range(half))
    for k in range(NUM_BUCKETS - half):
        starts.append(math.ceil(half * (MAX_DISTANCE / half) ** (k / (NUM_BUCKETS - half)) - 1e-9))
    return starts


def _fill_bias(thr_ref, dist, value_of_bucket):
    def body(b, acc):
        return jnp.where(dist >= thr_ref[b], value_of_bucket(b), acc)
    init = jnp.zeros(dist.shape, F32) + value_of_bucket(0)
    return lax.fori_loop(1, NUM_BUCKETS, body, init)


def _bias_tiles_kernel(thr_ref, rb_ref, ta_ref, tc_ref, sa_ref, scb_ref, scc_ref, scbn_ref, sccn_ref,
                       *, att_tile, past, dec, win_rows, chunk):
    T = att_tile
    key = lax.broadcasted_iota(jnp.int32, (T, T), 0)
    qry = lax.broadcasted_iota(jnp.int32, (T, T), 1)
    for delta in range(2):
        dist = delta * T + qry - key
        for h in range(H_A):
            b = _fill_bias(thr_ref, dist, lambda k, h=h: rb_ref[k, h]) - rb_ref[NUM_BUCKETS - 1, h]
            ta_ref[h, delta] = jnp.where(dist >= 0, b * LOG2E, NEG_INF)
    r = lax.broadcasted_iota(jnp.int32, (BAND, 2 * BAND), 0)
    c = lax.broadcasted_iota(jnp.int32, (BAND, 2 * BAND), 1)
    dcls = BAND + r - c
    ok = (dcls >= 0) & (dcls <= BAND)
    for bi, (_, dil) in enumerate(DILATED):
        for h in range(H_C):
            b = _fill_bias(thr_ref, dcls * dil, lambda k, h=h: rb_ref[k, H_A + h])
            tc_ref[bi, h] = jnp.where(ok, b, NEG_INF)
    r = lax.broadcasted_iota(jnp.int32, (2 * G_A * dec, PAGE_SIZE), 0)
    c = lax.broadcasted_iota(jnp.int32, (2 * G_A * dec, PAGE_SIZE), 1)
    tok = r % dec
    first_head = r % (G_A * dec) < dec
    for g in range(KV_A):
        val = lambda k, g=g: jnp.where(first_head, rb_ref[k, G_A * g], rb_ref[k, G_A * g + 1])
        sa_ref[0, g] = _fill_bias(thr_ref, jnp.full(r.shape, 2 * PAGE_SIZE, jnp.int32), val)
        sa_ref[1, g] = _fill_bias(thr_ref, PAGE_SIZE + tok - c, val)
        sa_ref[2, g] = jnp.where((c <= tok) & (c < dec), _fill_bias(thr_ref, tok - c, val), NEG_INF)
    def count(dist):
        n = jnp.zeros(dist.shape, F32)
        for window, dil in DILATED:
            n = n + jnp.where((dist >= 0) & (dist <= window) & (dist % dil == 0), 1.0, 0.0)
        return n
    head_col = lax.broadcasted_iota(jnp.int32, (H_C * dec, 1), 0) // dec

    def head_bias(k):
        col = jnp.zeros((H_C * dec, 1), F32)
        for h in range(H_C):
            col = jnp.where(head_col == h, rb_ref[k, H_A + h], col)
        return col

    def pair_tile(n_cols, row_dist):
        r = lax.broadcasted_iota(jnp.int32, (H_C * dec, n_cols), 0)
        c = lax.broadcasted_iota(jnp.int32, (H_C * dec, n_cols), 1)
        dist = row_dist(r % dec, c // H_C)
        n = jnp.where(r // dec == c % H_C, count(dist), 0.0)
        return n, jnp.where(n > 0, _fill_bias(thr_ref, dist, head_bias), NEG_INF)

    for ch in range(win_rows // chunk):
        n, b = pair_tile(chunk * H_C, lambda t, w, ch=ch: win_rows + t - (ch * chunk + w))
        scc_ref[ch] = n
        scb_ref[ch] = b
    n, b = pair_tile(dec * H_C, lambda t, w: t - w)
    sccn_ref[...] = n
    scbn_ref[...] = b


def _bias_tiles(rel_bias, past, dec, win_rows):
    T = ATT_TILE
    thr = jnp.asarray(_bucket_starts(), jnp.int32)
    nch = win_rows // WIN_CHUNK
    out_shape = (
        jax.ShapeDtypeStruct((H_A, 2, T, T), F32),
        jax.ShapeDtypeStruct((len(DILATED), H_C, BAND, 2 * BAND), F32),
        jax.ShapeDtypeStruct((3, KV_A, 2 * G_A * dec, PAGE_SIZE), F32),
        jax.ShapeDtypeStruct((nch, H_C * dec, WIN_CHUNK * H_C), F32),
        jax.ShapeDtypeStruct((nch, H_C * dec, WIN_CHUNK * H_C), F32),
        jax.ShapeDtypeStruct((H_C * dec, H_C * dec), F32),
        jax.ShapeDtypeStruct((H_C * dec, H_C * dec), F32),
    )
    smem = pl.BlockSpec(memory_space=pltpu.SMEM)
    return pl.pallas_call(
        functools.partial(_bias_tiles_kernel, att_tile=T, past=past, dec=dec, win_rows=win_rows, chunk=WIN_CHUNK),
        in_specs=[smem, smem],
        out_shape=out_shape,
        compiler_params=pltpu.CompilerParams(vmem_limit_bytes=VMEM_LIMIT),
        name="bias_tiles",
    )(thr, rel_bias)


def _ab_proj_kernel(h_ref, lng_ref, win_ref, gcq_ref, wuq_ref, wukt_ref, gckv_ref, cos_ref, sin_ref,
                    qa_ref, ka_ref, va_ref, ckv_ref, kr_ref, kabf_ref, vabf_ref, kcat_ref, qb_ref, *, q_mult):
    xn = _rms(h_ref[...], lng_ref[...]).astype(BF16)
    z = _dot(xn, win_ref[...])
    qa_ref[...] = (z[:, 0:512] * (DK_A ** -0.5 * q_mult)).astype(BF16)
    ka = z[:, 512:768]
    va = z[:, 768:1024]
    ka_ref[...] = ka
    va_ref[...] = va
    kabf_ref[...] = ka.astype(BF16)
    vabf_ref[...] = va.astype(BF16)
    cos = cos_ref[...]
    sin = sin_ref[...]
    ckv = _rms(z[:, 1280:1408], gckv_ref[...])
    kr = z[:, 1408:1536] * cos + z[:, 1536:1664] * sin
    ckv_ref[...] = ckv
    kr_ref[...] = kr[:, 0:DR_B]
    kcat_ref[:, 0:128] = ckv.astype(BF16)
    kcat_ref[:, 128:256] = kr.astype(BF16)
    cqn = _rms(z[:, 1024:1280], gcq_ref[...]).astype(BF16)
    y = _dot(cqn, wuq_ref[...])
    scale = (DN_B + DR_B) ** -0.5 * q_mult
    for h in range(H_B):
        q_lat = _dot(y[:, h * 256:h * 256 + 128].astype(BF16), wukt_ref[h])
        q_rope = y[:, h * 256 + 128:h * 256 + 256] * cos + y[:, 1024 + h * 128:1152 + h * 128] * sin
        qb_ref[:, h * 256:h * 256 + 128] = (q_lat * scale).astype(BF16)
        qb_ref[:, h * 256 + 128:h * 256 + 256] = (q_rope * scale).astype(BF16)


def _ab_proj(h, lng, w_in, g_cq, w_uq, w_ukt, g_ckv, cos, sin, q_mult):
    M = h.shape[0]
    tm = min(ROW_TILE, M)
    period = cos.shape[0] // tm
    row = lambda n: pl.BlockSpec((tm, n), lambda i: (i, 0))
    full = lambda a: pl.BlockSpec(a.shape, lambda i: (0,) * a.ndim)
    rot = pl.BlockSpec((tm, 128), lambda i: (i % period, 0))
    outs = ((512, BF16), (256, F32), (256, F32), (128, F32), (64, F32), (256, BF16), (256, BF16), (256, BF16),
            (1024, BF16))
    return pl.pallas_call(
        functools.partial(_ab_proj_kernel, q_mult=q_mult),
        grid=(M // tm,),
        in_specs=[row(D_MODEL), full(lng), full(w_in), full(g_cq), full(w_uq), full(w_ukt), full(g_ckv), rot, rot],
        out_specs=[row(n) for n, _ in outs],
        out_shape=[jax.ShapeDtypeStruct((M, n), dt) for n, dt in outs],
        compiler_params=_cparams(1),
        name="ab_proj",
    )(h, lng, w_in, g_cq, w_uq, w_ukt, g_ckv, cos, sin)


def _diff_lambda(lq1_ref, lk1_ref, lq2_ref, lk2_ref, lam_init):
    a = jnp.sum(lq1_ref[...] * lk1_ref[...], axis=-1, keepdims=True)
    b = jnp.sum(lq2_ref[...] * lk2_ref[...], axis=-1, keepdims=True)
    return jnp.exp(a) - jnp.exp(b) + lam_init


def _softmax_steps(scores, vt, m_ref, l_ref, acc_ref):
    ps, alphas = [], []
    for i, s in enumerate(scores):
        m_prev = m_ref[i]
        m_new = jnp.maximum(m_prev, jnp.max(s, axis=0, keepdims=True))
        alpha = jnp.exp2(m_prev - m_new)
        p = jnp.exp2(s - m_new)
        l_ref[i] = alpha * l_ref[i] + jnp.sum(p, axis=0, keepdims=True)
        m_ref[i] = m_new
        ps.append(p.astype(BF16))
        alphas.append(alpha)
    pvs = [_dot(vt, p) for p in ps]
    for i, pv in enumerate(pvs):
        acc_ref[i] = alphas[i] * acc_ref[i] + pv


def _flash_a_kernel(qt_ref, k_ref, vt_ref, bt_ref, lq1_ref, lk1_ref, lq2_ref, lk2_ref, gh_ref,
                    o_ref, m_ref, l_ref, acc_ref, *, T, lam_init):
    qi = pl.program_id(2)
    m_ref[...] = jnp.full(m_ref.shape, NEG_INF, F32)
    l_ref[...] = jnp.zeros(l_ref.shape, F32)
    acc_ref[...] = jnp.zeros(acc_ref.shape, F32)

    def update(kb, bias_of_head):
        start = pl.multiple_of(kb * T, T)
        k = k_ref[pl.ds(start, T), :]
        vt = vt_ref[:, pl.ds(start, T)]
        scores = [_dot(k, qt_ref[c]) for c in range(2 * G_A)]
        if bias_of_head is not None:
            scores = [s + bias_of_head(c // 2) for c, s in enumerate(scores)]
        _softmax_steps(scores, vt, m_ref, l_ref, acc_ref)

    def far(kb, carry):
        update(kb, None)
        return carry

    lax.fori_loop(0, jnp.maximum(qi - 1, 0), far, 0)

    @pl.when(qi >= 1)
    def _():
        update(qi - 1, lambda hl: bt_ref[hl, 1])

    update(qi, lambda hl: bt_ref[hl, 0])

    lam = _diff_lambda(lq1_ref, lk1_ref, lq2_ref, lk2_ref, lam_init)
    for hl in range(G_A):
        o = acc_ref[2 * hl] / l_ref[2 * hl] - lam * (acc_ref[2 * hl + 1] / l_ref[2 * hl + 1])
        o = o * lax.rsqrt(jnp.mean(o * o, axis=0, keepdims=True) + EPS) * gh_ref[...] * (1.0 - lam_init)
        o_ref[:, hl * 128:(hl + 1) * 128] = o.T.astype(o_ref.dtype)


def _flash_a(qa, ka_bf, va_bf, tiles_a, lq1, lk1, lq2, lk2, g_head, batch, seq, lam_init):
    T = ATT_TILE
    nq = seq // T
    q5 = jnp.transpose(qa.reshape(batch, seq, KV_A, G_A, 2 * DK_A), (0, 2, 3, 4, 1))
    first = (jnp.arange(2 * DK_A) < DK_A)[:, None]
    zero = jnp.zeros((), qa.dtype)
    qt = jnp.stack([jnp.where(first, q5, zero), jnp.where(first, zero, q5)], axis=3)
    qt = qt.reshape(batch, KV_A, 2 * G_A, 2 * DK_A, seq)
    vt = jnp.transpose(va_bf.reshape(batch, seq, KV_A, DV_A), (0, 2, 3, 1))
    vec = pl.BlockSpec((1, DK_A), lambda b, g, i: (0, 0))
    return pl.pallas_call(
        functools.partial(_flash_a_kernel, T=T, lam_init=lam_init),
        grid=(batch, KV_A, nq),
        in_specs=[
            pl.BlockSpec((None, None, 2 * G_A, 2 * DK_A, T), lambda b, g, i: (b, g, 0, 0, i)),
            pl.BlockSpec((seq, 128), lambda b, g, i: (b, g)),
            pl.BlockSpec((None, None, DV_A, seq), lambda b, g, i: (b, g, 0, 0)),
            pl.BlockSpec((G_A, 2, T, T), lambda b, g, i: (g, 0, 0, 0)),
            vec, vec, vec, vec,
            pl.BlockSpec((DV_A, 1), lambda b, g, i: (0, 0)),
        ],
        out_specs=pl.BlockSpec((T, 256), lambda b, g, i: (b * nq + i, g)),
        out_shape=jax.ShapeDtypeStruct((batch * seq, H_A * DV_A), BF16),
        scratch_shapes=[
            pltpu.VMEM((2 * G_A, 1, T), F32),
            pltpu.VMEM((2 * G_A, 1, T), F32),
            pltpu.VMEM((2 * G_A, DV_A, T), F32),
        ],
        compiler_params=_cparams(3),
        name="flash_a",
    )(qt, ka_bf, vt, tiles_a, lq1, lk1, lq2, lk2, g_head.reshape(DV_A, 1))


def _flash_b_kernel(qt_ref, kc_ref, ct_ref, wuv_ref, o_ref, m_ref, l_ref, acc_ref, *, T):
    qi = pl.program_id(1)
    m_ref[...] = jnp.full(m_ref.shape, NEG_INF, F32)
    l_ref[...] = jnp.zeros(l_ref.shape, F32)
    acc_ref[...] = jnp.zeros(acc_ref.shape, F32)
    key = lax.broadcasted_iota(jnp.int32, (T, T), 0)
    qry = lax.broadcasted_iota(jnp.int32, (T, T), 1)

    def update(kb, diagonal):
        start = pl.multiple_of(kb * T, T)
        kc = kc_ref[pl.ds(start, T), :]
        ct = ct_ref[:, pl.ds(start, T)]
        scores = [_dot(kc, qt_ref[h]) for h in range(H_B)]
        if diagonal:
            scores = [jnp.where(qry >= key, s, NEG_INF) for s in scores]
        _softmax_steps(scores, ct, m_ref, l_ref, acc_ref)

    def far(kb, carry):
        update(kb, False)
        return carry

    lax.fori_loop(0, qi, far, 0)
    update(qi, True)
    for h in range(H_B):
        o_lat = (acc_ref[h] / l_ref[h]).T.astype(BF16)
        o_ref[:, h * 128:(h + 1) * 128] = _dot(o_lat, wuv_ref[h]).astype(o_ref.dtype)


def _flash_b(qb, kcat, w_uvt, batch, seq):
    T = ATT_TILE
    nq = seq // T
    qt = jnp.transpose(qb.reshape(batch, seq, H_B, 256), (0, 2, 3, 1))
    ct = jnp.transpose(kcat[:, 0:KV_RANK].reshape(batch, seq, KV_RANK), (0, 2, 1))
    return pl.pallas_call(
        functools.partial(_flash_b_kernel, T=T),
        grid=(batch, nq),
        in_specs=[
            pl.BlockSpec((None, H_B, 256, T), lambda b, i: (b, 0, 0, i)),
            pl.BlockSpec((seq, 256), lambda b, i: (b, 0)),
            pl.BlockSpec((None, KV_RANK, seq), lambda b, i: (b, 0, 0)),
            pl.BlockSpec((H_B, KV_RANK, DV_B), lambda b, i: (0, 0, 0)),
        ],
        out_specs=pl.BlockSpec((T, H_B * DV_B), lambda b, i: (b * nq + i, 0)),
        out_shape=jax.ShapeDtypeStruct((batch * seq, H_B * DV_B), BF16),
        scratch_shapes=[
            pltpu.VMEM((H_B, 1, T), F32),
            pltpu.VMEM((H_B, 1, T), F32),
            pltpu.VMEM((H_B, KV_RANK, T), F32),
        ],
        compiler_params=_cparams(2),
        name="flash_b",
    )(qt, kcat, ct, w_uvt)


def _sample_ab_kernel(pt_ref, qa_ref, qb_ref, knew_ref, vnew_ref, kcn_ref, ba_ref, lq1_ref, lk1_ref,
                      lq2_ref, lk2_ref, gh_ref, wuv_ref, *rest, P, dec, lam_init):
    kp = rest[0:P]
    vp = rest[P:2 * P]
    cp = rest[2 * P:3 * P]
    rp = rest[3 * P:4 * P]
    oa_ref, ob_ref, ma_ref, la_ref, acca_ref, mb_ref, lb_ref, accb_ref = rest[4 * P:]
    j = pl.program_id(1)
    last = j == pl.num_programs(1) - 1
    R = 2 * dec

    @pl.when(j == 0)
    def _():
        ma_ref[...] = jnp.full(ma_ref.shape, NEG_INF, F32)
        la_ref[...] = jnp.zeros(la_ref.shape, F32)
        acca_ref[...] = jnp.zeros(acca_ref.shape, F32)
        mb_ref[...] = jnp.full(mb_ref.shape, NEG_INF, F32)
        lb_ref[...] = jnp.zeros(lb_ref.shape, F32)
        accb_ref[...] = jnp.zeros(accb_ref.shape, F32)

    def online(s_list, v_list, m_ref, l_ref, acc_ref, idx):
        s = jnp.concatenate(s_list, axis=1)
        m_prev = m_ref[idx]
        m_new = jnp.maximum(m_prev, jnp.max(s, axis=-1, keepdims=True))
        alpha = jnp.exp(m_prev - m_new)
        p = jnp.exp(s - m_new)
        pv = _dot(p[:, 0:PAGE_SIZE].astype(BF16), v_list[0])
        for n in range(1, len(v_list)):
            pv = pv + _dot(p[:, n * PAGE_SIZE:(n + 1) * PAGE_SIZE].astype(BF16), v_list[n])
        l_ref[idx] = alpha * l_ref[idx] + jnp.sum(p, axis=-1, keepdims=True)
        acc_ref[idx] = alpha * acc_ref[idx] + pv
        m_ref[idx] = m_new

    for g in range(KV_A):
        v_list = [vp[n][pl.ds(g, PAGE_SIZE, stride=KV_A), :].astype(BF16) for n in range(P)]
        far = ba_ref[0, g]
        near = jnp.where(last, ba_ref[1, g], far)
        s_list = []
        for n in range(P):
            s = jnp.concatenate([_dot(qa_ref[0, g, mp], kp[n][g, mp].astype(BF16)) for mp in range(2)], axis=0)
            s_list.append(s + (near if n == P - 1 else far))
        online(s_list, v_list, ma_ref, la_ref, acca_ref, g)

    qb = qb_ref[0]
    c_list = [cp[n][...].astype(BF16) for n in range(P)]
    s_list = [_dot_nt(qb[:, 0:KV_RANK], c_list[n])
              + _dot(qb[:, KV_RANK:KV_RANK + DR_B], rp[n][...].astype(BF16)) for n in range(P)]
    online(s_list, c_list, mb_ref, lb_ref, accb_ref, 0)

    @pl.when(last)
    def _():
        lane_a = lax.broadcasted_iota(jnp.int32, (2 * R, PAGE_SIZE), 1)
        row_a = lax.broadcasted_iota(jnp.int32, (2 * R, DK_A), 0)

        def new_keys(qf, key_row, vals, bias, m_ref, l_ref, acc_ref, idx, lane):
            s = bias
            for t in range(dec):
                col = jnp.sum(qf * key_row(t), axis=-1, keepdims=True)
                s = s + jnp.where(lane == t, col, 0.0)
            m_prev = m_ref[idx]
            m_new = jnp.maximum(m_prev, jnp.max(s, axis=-1, keepdims=True))
            alpha = jnp.exp(m_prev - m_new)
            p = jnp.exp(s - m_new)
            pv = p[:, 0:1] * vals[0:1, :]
            for t in range(1, dec):
                pv = pv + p[:, t:t + 1] * vals[t:t + 1, :]
            l_ref[idx] = alpha * l_ref[idx] + jnp.sum(p, axis=-1, keepdims=True)
            acc_ref[idx] = alpha * acc_ref[idx] + pv
            m_ref[idx] = m_new

        lam = _diff_lambda(lq1_ref, lk1_ref, lq2_ref, lk2_ref, lam_init)
        for g in range(KV_A):
            qf = jnp.concatenate([qa_ref[0, g, 0], qa_ref[0, g, 1]], axis=0).astype(F32)
            key_row = lambda t, g=g: jnp.where(row_a < R, knew_ref[0, g, 0, t:t + 1, :], knew_ref[0, g, 1, t:t + 1, :])
            new_keys(qf, key_row, vnew_ref[0, g], ba_ref[2, g], ma_ref, la_ref, acca_ref, g, lane_a)
            on = acca_ref[g] / la_ref[g]
            o = on[0:R] - lam * on[R:2 * R]
            oa_ref[0, g] = (_rms(o, gh_ref[...]) * (1.0 - lam_init)).astype(oa_ref.dtype)

        RB = H_B * dec
        row = lax.broadcasted_iota(jnp.int32, (RB, PAGE_SIZE), 0)
        lane_b = lax.broadcasted_iota(jnp.int32, (RB, PAGE_SIZE), 1)
        bias_b = jnp.where((lane_b <= row % dec) & (lane_b < dec), 0.0, NEG_INF)
        kcn = kcn_ref[0]
        new_keys(qb.astype(F32), lambda t: kcn[t:t + 1, :], kcn[:, 0:KV_RANK], bias_b, mb_ref, lb_ref, accb_ref, 0,
                 lane_b)
        o_lat = (accb_ref[0] / lb_ref[0]).astype(BF16)
        out = jnp.zeros((RB, DV_B), F32)
        for h in range(H_B):
            out = out + jnp.where(row // dec == h, _dot(o_lat, wuv_ref[h]), 0.0)
        ob_ref[0] = out.astype(ob_ref.dtype)


def _sample_ab(layer, page_table, qa, qb, knew, vnew, kcn, bias_a, lq1, lk1, lq2, lk2, g_head, w_uvt,
               cache_k, cache_v, cache_c, cache_r, dec, lam_init):
    n_seq, n_pages = page_table.shape
    P = PAGES_PER_STEP
    R = 2 * dec
    const = lambda a: pl.BlockSpec(a.shape, lambda n, j, pt: (0,) * a.ndim)
    seq_blk = lambda a: pl.BlockSpec((1,) + a.shape[1:], lambda n, j, pt: (n,) + (0,) * (a.ndim - 1))

    def page_spec(a, p):
        nd = a.ndim - 2
        return pl.BlockSpec((None, None) + a.shape[2:],
                            lambda n, j, pt: (layer, pt[n, j * P + p]) + (0,) * nd)

    pages = ([page_spec(cache_k, p) for p in range(P)] + [page_spec(cache_v, p) for p in range(P)]
             + [page_spec(cache_c, p) for p in range(P)] + [page_spec(cache_r, p) for p in range(P)])
    args = (qa, qb, knew, vnew, kcn, bias_a, lq1, lk1, lq2, lk2, g_head, w_uvt)
    in_specs = [seq_blk(qa), seq_blk(qb), seq_blk(knew), seq_blk(vnew), seq_blk(kcn), const(bias_a),
                const(lq1), const(lk1), const(lq2), const(lk2), const(g_head), const(w_uvt)] + pages
    grid_spec = pltpu.PrefetchScalarGridSpec(
        num_scalar_prefetch=1,
        grid=(n_seq, n_pages // P),
        in_specs=in_specs,
        out_specs=[pl.BlockSpec((1, KV_A, R, DV_A), lambda n, j, pt: (n, 0, 0, 0)),
                   pl.BlockSpec((1, H_B * dec, DV_B), lambda n, j, pt: (n, 0, 0))],
        scratch_shapes=[
            pltpu.VMEM((KV_A, 2 * R, 1), F32), pltpu.VMEM((KV_A, 2 * R, 1), F32),
            pltpu.VMEM((KV_A, 2 * R, DV_A), F32),
            pltpu.VMEM((1, H_B * dec, 1), F32), pltpu.VMEM((1, H_B * dec, 1), F32),
            pltpu.VMEM((1, H_B * dec, KV_RANK), F32),
        ],
    )
    return pl.pallas_call(
        functools.partial(_sample_ab_kernel, P=P, dec=dec, lam_init=lam_init),
        grid_spec=grid_spec,
        out_shape=[jax.ShapeDtypeStruct((n_seq, KV_A, R, DV_A), BF16),
                   jax.ShapeDtypeStruct((n_seq, H_B * dec, DV_B), BF16)],
        compiler_params=_cparams(2),
        name="sample_ab",
    )(page_table, *args, *([cache_k] * P + [cache_v] * P + [cache_c] * P + [cache_r] * P))


def _out_proj_kernel(a_ref, b_ref, w_ref, res_ref, o_ref):
    ka = a_ref.shape[1]
    y = _dot(a_ref[...], w_ref[0:ka, :]) + _dot(b_ref[...], w_ref[ka:, :])
    o_ref[...] = res_ref[...] + y


def _out_proj(a, b, w, res):
    M = res.shape[0]
    tm = min(ROW_TILE, M)
    row = lambda n: pl.BlockSpec((tm, n), lambda i: (i, 0))
    return pl.pallas_call(
        _out_proj_kernel,
        grid=(M // tm,),
        in_specs=[row(a.shape[1]), row(b.shape[1]), pl.BlockSpec(w.shape, lambda i: (0, 0)), row(D_MODEL)],
        out_specs=row(D_MODEL),
        out_shape=jax.ShapeDtypeStruct((M, D_MODEL), F32),
        compiler_params=_cparams(1),
        name="out_proj",
    )(a, b, w, res)


def _c_proj_kernel(*refs, dils):
    h_ref, g_ref, w_ref, q_ref, k_ref, v_ref = refs[0:6]
    n = H_C * DH_C
    xn = _rms(h_ref[...], g_ref[...]).astype(BF16)
    z = _dot(xn, w_ref[...])
    q = z[:, 0:n] * (DH_C ** -0.5)
    q_ref[...] = q.astype(BF16)
    k_ref[...] = z[:, n:2 * n]
    v_ref[...] = z[:, 2 * n:3 * n]
    if not dils:
        return
    kbf_ref, vbf_ref = refs[6:8]
    cls_refs = refs[8:8 + 3 * len(dils)]
    zs_ref = refs[8 + 3 * len(dils)]
    kbf_ref[...] = z[:, n:2 * n].astype(BF16)
    vbf_ref[...] = z[:, 2 * n:3 * n].astype(BF16)
    for j in range(H_C):
        zs_ref[j] = q[:, j * DH_C:(j + 1) * DH_C]
    for j in range(H_C, 3 * H_C):
        zs_ref[j] = z[:, j * DH_C:(j + 1) * DH_C]
    tm = zs_ref.shape[1]
    for di, dil in enumerate(dils):
        for part in range(3):
            o_ref = cls_refs[3 * di + part]
            for r in range(dil):
                for j in range(H_C):
                    rows = zs_ref[part * H_C + j, pl.ds(r, tm // dil, stride=dil), :]
                    o_ref[r, :, j * DH_C:(j + 1) * DH_C] = rows.astype(BF16)


def _c_proj(h, g, w, batch=None, seq=None, dils=()):
    M = h.shape[0]
    tm = min(ROW_TILE, M)
    n = H_C * DH_C
    row = lambda c: pl.BlockSpec((tm, c), lambda i: (i, 0))
    out_specs = [row(n)] * 3
    out_shape = [jax.ShapeDtypeStruct((M, n), dt) for dt in (BF16, F32, F32)]
    scratch = []
    if dils:
        tiles = seq // tm
        out_specs += [row(n)] * 2
        out_shape += [jax.ShapeDtypeStruct((M, n), BF16)] * 2
        for dil in dils:
            out_specs += [pl.BlockSpec((None, dil, tm // dil, n), lambda i: (i // tiles, 0, i % tiles, 0))] * 3
            out_shape += [jax.ShapeDtypeStruct((batch, dil, seq // dil, n), BF16)] * 3
        scratch = [pltpu.VMEM((3 * H_C, tm, DH_C), F32)]
    return pl.pallas_call(
        functools.partial(_c_proj_kernel, dils=tuple(dils)),
        grid=(M // tm,),
        in_specs=[row(D_MODEL), pl.BlockSpec(g.shape, lambda i: (0, 0)), pl.BlockSpec(w.shape, lambda i: (0, 0))],
        out_specs=out_specs,
        out_shape=out_shape,
        scratch_shapes=scratch,
        compiler_params=_cparams(1),
        name="c_proj_classes" if dils else "c_proj",
    )(h, g, w)


def _dilated_kernel(q_ref, kp_ref, kc_ref, vp_ref, vc_ref, bt_ref, o_ref, lse_ref):
    has_prev = pl.program_id(2) > 0
    heads = [slice(h * DH_C, (h + 1) * DH_C) for h in range(H_C)]
    scores = [(_dot_nt(q_ref[:, sl], kp_ref[:, sl]), _dot_nt(q_ref[:, sl], kc_ref[:, sl])) for sl in heads]
    probs = []
    for h, (sp, sc) in enumerate(scores):
        bt = bt_ref[h]
        sp = jnp.where(has_prev, sp + bt[:, 0:BAND], NEG_INF)
        sc = sc + bt[:, BAND:2 * BAND]
        m = jnp.maximum(jnp.max(sp, axis=-1, keepdims=True), jnp.max(sc, axis=-1, keepdims=True))
        pp = jnp.exp(sp - m)
        pc = jnp.exp(sc - m)
        l = jnp.sum(pp, axis=-1, keepdims=True) + jnp.sum(pc, axis=-1, keepdims=True)
        lse_ref[:, h * LSE_W:(h + 1) * LSE_W] = jnp.broadcast_to(m + jnp.log(l), (BAND, LSE_W))
        probs.append((pp.astype(BF16), pc.astype(BF16), l))
    outs = [_dot(pp, vp_ref[:, sl]) + _dot(pc, vc_ref[:, sl]) for (pp, pc, _), sl in zip(probs, heads)]
    for o, (_, _, l), sl in zip(outs, probs, heads):
        o_ref[:, sl] = (o / l).astype(o_ref.dtype)


def _dilated_branch(q, k, v, tiles_c, branch, batch, seq):
    window, dil = DILATED[branch]
    assert window // dil == BAND and seq % (dil * BAND) == 0
    L = seq // dil
    nb = L // BAND
    n = H_C * DH_C
    cur = pl.BlockSpec((None, None, BAND, n), lambda b, r, i: (b, r, i, 0))
    prev = pl.BlockSpec((None, None, BAND, n), lambda b, r, i: (b, r, jnp.maximum(i - 1, 0), 0))
    return pl.pallas_call(
        _dilated_kernel,
        grid=(batch, dil, nb),
        in_specs=[cur, prev, cur, prev, cur,
                  pl.BlockSpec((None, H_C, BAND, 2 * BAND), lambda b, r, i: (branch, 0, 0, 0))],
        out_specs=[cur, pl.BlockSpec((None, None, BAND, H_C * LSE_W), lambda b, r, i: (b, r, i, 0))],
        out_shape=[jax.ShapeDtypeStruct((batch, dil, L, n), F32),
                   jax.ShapeDtypeStruct((batch, dil, L, H_C * LSE_W), F32)],
        compiler_params=_cparams(3),
        name=f"dilated_{dil}",
    )(q, k, k, v, v, tiles_c)


def _c_out_kernel(*refs, dils):
    nb = len(dils)
    o_refs = list(refs[0:nb])
    l_refs = list(refs[nb:2 * nb])
    w_ref, res_ref, out_ref, mix_ref = refs[2 * nb:2 * nb + 4]
    nat = refs[2 * nb + 4:]
    tm = mix_ref.shape[0]
    pos = 0
    for b, dil in enumerate(dils):
        if dil == 1:
            continue
        on_ref, ln_ref = nat[pos:pos + 2]
        pos += 2
        for r in range(dil):
            rows = pl.ds(r, tm // dil, stride=dil)
            for h in range(H_C):
                on_ref[h, rows, :] = o_refs[b][r, :, h * DH_C:(h + 1) * DH_C]
            ln_ref[rows, :] = l_refs[b][r]
        o_refs[b], l_refs[b] = on_ref, ln_ref
    ls = tuple(l[...] for l in l_refs)
    m = jnp.maximum(jnp.maximum(ls[0], ls[1]), ls[2])
    e = [jnp.exp(l - m) for l in ls]
    den = e[0] + e[1] + e[2]
    wts = [x / den for x in e]
    for h in range(H_C):
        sl = slice(h * DH_C, (h + 1) * DH_C)
        mix = None
        for b, dil in enumerate(dils):
            o = o_refs[b][:, sl] if dil == 1 else o_refs[b][h]
            term = wts[b][:, h * LSE_W:h * LSE_W + 1] * o
            mix = term if mix is None else mix + term
        mix_ref[:, sl] = mix.astype(BF16)
    out_ref[...] = res_ref[...] + _dot(mix_ref[...], w_ref[...])


def _c_out(outs, lses, w, res, seq):
    M = res.shape[0]
    tm = 256
    n = H_C * DH_C
    tiles = seq // tm
    dils = tuple(dil for _, dil in DILATED)
    row = lambda c: pl.BlockSpec((tm, c), lambda i: (i, 0))

    def cls(c, dil):
        if dil == 1:
            return pl.BlockSpec((None, None, tm, c), lambda i: (i // tiles, 0, i % tiles, 0))
        return pl.BlockSpec((None, dil, tm // dil, c), lambda i: (i // tiles, 0, i % tiles, 0))

    scratch = [pltpu.VMEM((tm, n), BF16)]
    for dil in dils:
        if dil > 1:
            scratch += [pltpu.VMEM((H_C, tm, DH_C), F32), pltpu.VMEM((tm, H_C * LSE_W), F32)]
    return pl.pallas_call(
        functools.partial(_c_out_kernel, dils=dils),
        grid=(M // tm,),
        in_specs=([cls(n, dil) for dil in dils] + [cls(H_C * LSE_W, dil) for dil in dils]
                  + [pl.BlockSpec(w.shape, lambda i: (0, 0)), row(D_MODEL)]),
        out_specs=row(D_MODEL),
        out_shape=jax.ShapeDtypeStruct((M, D_MODEL), F32),
        scratch_shapes=scratch,
        compiler_params=_cparams(1),
        name="c_out",
    )(*outs, *lses, w, res)


def _sample_c_kernel(q_ref, knew_ref, vnew_ref, kst_ref, vst_ref, knx_ref, vnx_ref, bm_ref, cnt_ref,
                     bmn_ref, cntn_ref, prev_k_ref, prev_v_ref, o_ref, ok_ref, ov_ref, m_ref, l_ref, acc_ref,
                     *, dec, chunk):
    del prev_k_ref, prev_v_ref
    c = pl.program_id(1)
    last = c == pl.num_programs(1) - 1

    @pl.when(c == 0)
    def _():
        m_ref[...] = jnp.full(m_ref.shape, NEG_INF, F32)
        l_ref[...] = jnp.zeros(l_ref.shape, F32)
        acc_ref[...] = jnp.zeros(acc_ref.shape, F32)

    q = q_ref[0]
    kf = kst_ref[0, 0].reshape(chunk * H_C, DH_C).astype(BF16)
    vf = vst_ref[0, 0].reshape(chunk * H_C, DH_C).astype(BF16)
    s = _dot_nt(q, kf) + bm_ref[c]
    m_prev = m_ref[...]
    m_new = jnp.maximum(m_prev, jnp.max(s, axis=-1, keepdims=True))
    m_use = jnp.where(m_new == NEG_INF, 0.0, m_new)
    alpha = jnp.exp(m_prev - m_use)
    w = cnt_ref[c] * jnp.exp(s - m_use)
    l_ref[...] = alpha * l_ref[...] + jnp.sum(w, axis=-1, keepdims=True)
    acc_ref[...] = alpha * acc_ref[...] + _dot(w.astype(BF16), vf)
    m_ref[...] = m_new

    ok_ref[0, 0, 0:chunk - dec] = kst_ref[0, 0, dec:chunk]
    ov_ref[0, 0, 0:chunk - dec] = vst_ref[0, 0, dec:chunk]

    @pl.when(jnp.logical_not(last))
    def _():
        ok_ref[0, 0, chunk - dec:chunk] = knx_ref[0, 0]
        ov_ref[0, 0, chunk - dec:chunk] = vnx_ref[0, 0]

    @pl.when(last)
    def _():
        ok_ref[0, 0, chunk - dec:chunk] = knew_ref[0]
        ov_ref[0, 0, chunk - dec:chunk] = vnew_ref[0]
        nk = dec * H_C
        knf = knew_ref[0].reshape(nk, DH_C).astype(BF16)
        vnf = vnew_ref[0].reshape(nk, DH_C).astype(BF16)
        sn = _dot_nt(q, knf) + bmn_ref[...]
        m_fin = jnp.maximum(m_new, jnp.max(sn, axis=-1, keepdims=True))
        a_fin = jnp.exp(m_new - m_fin)
        wn = cntn_ref[...] * jnp.exp(sn - m_fin)
        l_fin = a_fin * l_ref[...] + jnp.sum(wn, axis=-1, keepdims=True)
        o_ref[0] = ((a_fin * acc_ref[...] + _dot(wn.astype(BF16), vnf)) / l_fin).astype(o_ref.dtype)


def _sample_c(layer, q, knew, vnew, state_k, state_v, prev_k, prev_v, bm, cnt, bmn, cntn, dec):
    n_layers, n_seq, win_rows = state_k.shape[:3]
    chunk = WIN_CHUNK
    nch = win_rows // chunk
    n = H_C * DH_C
    seq_blk = lambda a: pl.BlockSpec((1,) + a.shape[1:], lambda s, c: (s,) + (0,) * (a.ndim - 1))
    const = lambda a: pl.BlockSpec(a.shape, lambda s, c: (0,) * a.ndim)
    st = pl.BlockSpec((1, 1, chunk, H_C, DH_C), lambda s, c: (layer, s, c, 0, 0))
    nxt = pl.BlockSpec((1, 1, dec, H_C, DH_C),
                       lambda s, c: (layer, s, jnp.minimum((c + 1) * (chunk // dec), win_rows // dec - 1), 0, 0))
    in_specs = [seq_blk(q), seq_blk(knew), seq_blk(vnew), st, st, nxt, nxt,
                const(bm), const(cnt), const(bmn), const(cntn)]
    args = [q, knew, vnew, state_k, state_v, state_k, state_v, bm, cnt, bmn, cntn]
    aliases = {}
    if prev_k is not None:
        in_specs += [pl.BlockSpec(memory_space=pl.ANY)] * 2
        aliases = {len(args): 1, len(args) + 1: 2}
        args += [prev_k, prev_v]
        kern = _sample_c_kernel
    else:
        kern = lambda *refs, **kw: _sample_c_kernel(*refs[:11], None, None, *refs[11:], **kw)
    win_shape = jax.ShapeDtypeStruct((n_layers, n_seq, win_rows, H_C, DH_C), F32)
    return pl.pallas_call(
        functools.partial(kern, dec=dec, chunk=chunk),
        grid=(n_seq, nch),
        in_specs=in_specs,
        out_specs=[pl.BlockSpec((1, H_C * dec, DH_C), lambda s, c: (s, 0, 0)), st, st],
        out_shape=[jax.ShapeDtypeStruct((n_seq, H_C * dec, DH_C), BF16), win_shape, win_shape],
        scratch_shapes=[pltpu.VMEM((H_C * dec, 1), F32), pltpu.VMEM((H_C * dec, 1), F32),
                        pltpu.VMEM((H_C * dec, DH_C), F32)],
        input_output_aliases=aliases,
        compiler_params=_cparams(2),
        name="sample_c",
    )(*args)


def _ffn_kernel(*refs, tm, nf, seq_len, final_norm):
    short = seq_len < tm
    h_ref, g_ref, wg_ref, wu_ref, cw_ref, cb_ref, wd_ref = refs[0:7]
    pos = 7
    if short:
        s1_ref, s2_ref = refs[pos:pos + 2]
        pos += 2
    if final_norm:
        gf_ref = refs[pos]
        pos += 1
    out_ref, cst_ref, xn_ref, acc_ref, ext_ref = refs[pos:pos + 5]
    carry_ref = None if short else refs[pos + 5]
    i = pl.program_id(0)
    j = pl.program_id(1)

    @pl.when(j == 0)
    def _():
        xn_ref[...] = _rms(h_ref[...], g_ref[...]).astype(BF16)
        acc_ref[...] = jnp.zeros(acc_ref.shape, F32)

    xn = xn_ref[...]
    tf = wg_ref.shape[1]
    if short:
        ext_ref[0:8] = jnp.zeros((8, tf), F32)
        t = lax.broadcasted_iota(jnp.int32, (tm, 1), 0) % seq_len
    else:
        first = i % (seq_len // tm) == 0

        @pl.when(first)
        def _():
            ext_ref[0:8] = jnp.zeros((8, tf), F32)

        @pl.when(jnp.logical_not(first))
        def _():
            ext_ref[0:8] = carry_ref[j]

    down = None
    for c0 in range(0, tf, FF_GROUP):
        cols = slice(c0, min(c0 + FF_GROUP, tf))
        gate = _dot(xn, wg_ref[:, cols])
        up = _dot(xn, wu_ref[:, cols])
        ext_ref[8:8 + tm, cols] = gate
        if short:
            cst_ref[:, cols] = gate
        else:
            carry_ref[j, :, cols] = gate[tm - 8:tm]
            cst_ref[:, cols] = gate[tm - 8:tm]
        g1 = ext_ref[7:7 + tm, cols]
        g2 = ext_ref[6:6 + tm, cols]
        if short:
            g1 = jnp.where(t >= 1, g1, s1_ref[:, cols])
            g2 = jnp.where(t >= 2, g2, s2_ref[:, cols])
        conv = cb_ref[:, cols] + g2 * cw_ref[0:1, cols] + g1 * cw_ref[1:2, cols] + gate * cw_ref[2:3, cols]
        y = conv * jax.nn.sigmoid(conv) * up
        d = _dot(y.astype(BF16), wd_ref[cols, :])
        down = d if down is None else down + d
    acc_ref[...] += down

    @pl.when(j == nf - 1)
    def _():
        out = h_ref[...] + acc_ref[...]
        if final_norm:
            out = _rms(out, gf_ref[...])
        out_ref[...] = out


def _ffn(h, g, w_up, conv_w, conv_b, w_down, seq_len, shifted=None, final_gain=None):
    M = h.shape[0]
    tm = min(ROW_TILE, M)
    tf = FF_TILE
    nf = D_FF // tf
    short = seq_len < tm
    row = pl.BlockSpec((tm, D_MODEL), lambda i, j: (i, 0))
    in_specs = [row, pl.BlockSpec((1, D_MODEL), lambda i, j: (0, 0)),
                pl.BlockSpec((D_MODEL, tf), lambda i, j: (0, j)),
                pl.BlockSpec((D_MODEL, tf), lambda i, j: (0, nf + j)),
                pl.BlockSpec((CONV_W, tf), lambda i, j: (0, j)),
                pl.BlockSpec((1, tf), lambda i, j: (0, j)),
                pl.BlockSpec((tf, D_MODEL), lambda i, j: (j, 0))]
    args = [h, g, w_up, w_up, conv_w, conv_b, w_down]
    scratch = [pltpu.VMEM((tm, D_MODEL), BF16), pltpu.VMEM((tm, D_MODEL), F32), pltpu.VMEM((tm + 8, tf), F32)]
    if short:
        in_specs += [pl.BlockSpec((tm, tf), lambda i, j: (i, j))] * 2
        args += list(shifted)
        cst_spec = pl.BlockSpec((tm, tf), lambda i, j: (i, j))
        cst_shape = jax.ShapeDtypeStruct((M, D_FF), F32)
    else:
        cst_spec = pl.BlockSpec((None, 8, tf), lambda i, j: (i, 0, j))
        cst_shape = jax.ShapeDtypeStruct((M // tm, 8, D_FF), F32)
        scratch.append(pltpu.VMEM((nf, 8, tf), F32))
    if final_gain is not None:
        in_specs.append(pl.BlockSpec((1, D_MODEL), lambda i, j: (0, 0)))
        args.append(final_gain)
    return pl.pallas_call(
        functools.partial(_ffn_kernel, tm=tm, nf=nf, seq_len=seq_len, final_norm=final_gain is not None),
        grid=(M // tm, nf),
        in_specs=in_specs,
        out_specs=[row, cst_spec],
        out_shape=[jax.ShapeDtypeStruct((M, D_MODEL), F32), cst_shape],
        scratch_shapes=scratch,
        compiler_params=_cparams(2),
        name="ffn_short" if short else "ffn",
    )(*args)


def _rope_tables(pos):
    half = DR_B // 2
    inv = ROPE_BASE ** (-jnp.arange(half, dtype=F32) / half)
    ang = pos.astype(F32)[:, None] * inv[None, :]
    cos, sin = jnp.cos(ang), jnp.sin(ang)
    pad = jnp.zeros((pos.shape[0], 128 - DR_B), F32)
    return jnp.concatenate([cos, cos, pad], axis=1), jnp.concatenate([-sin, sin, pad], axis=1)


def _swap_halves(w):
    half = w.shape[-1] // 2
    return jnp.concatenate([w[..., half:], w[..., :half]], axis=-1)


def _prep_ab_weights(w_in, w_uq, w_uk, w_uv):
    kr = w_in[:, 1408:1472]
    z64 = jnp.zeros((D_MODEL, 64), F32)
    w_in_aug = jnp.concatenate([w_in[:, :1408], kr, z64, _swap_halves(kr), z64], axis=1).astype(BF16)
    nope = w_uq[:, :, :DN_B]
    rope = w_uq[:, :, DN_B:]
    zq = jnp.zeros((Q_RANK, H_B, 64), F32)
    main = jnp.concatenate([nope, rope, zq], axis=-1).reshape(Q_RANK, H_B * 256)
    swapped = jnp.concatenate([_swap_halves(rope), zq], axis=-1).reshape(Q_RANK, H_B * 128)
    w_uq_aug = jnp.concatenate([main, swapped], axis=1).astype(BF16)
    w_ukt = jnp.transpose(w_uk, (1, 2, 0)).astype(BF16)
    w_uvt = jnp.transpose(w_uv, (1, 0, 2)).astype(BF16)
    return w_in_aug, w_uq_aug, w_ukt, w_uvt


def kernel(x_prompt, x_sample, cache_a_k, cache_a_v, cache_mla_ckv, cache_mla_krope, state_win_k, state_win_v,
           state_conv, page_table, ln_mix, ln_ffn, ln_final, rel_bias, w_in_ab, lam_q1, lam_k1, lam_q2, lam_k2,
           g_head_a, g_cq, g_ckv, w_uq, w_uk, w_uv, w_out_ab, w_in_c, w_out_c, w_up, conv_w, conv_b, w_down):
    batch, seq, _ = x_prompt.shape
    n_seq, dec, _ = x_sample.shape
    n_pages = page_table.shape[1]
    past = n_pages * PAGE_SIZE
    depth = ln_mix.shape[0]
    win_rows = state_win_k.shape[2]
    n_pool = cache_a_k.shape[1]
    assert past == cache_a_k.shape[2] * n_pages and win_rows == WIN_MAX and past >= WIN_MAX
    assert seq >= WIN_MAX and seq % ROW_TILE == 0 and (n_seq * dec) % 8 == 0

    tiles_a, tiles_c, bias_sa, sc_bm, sc_cnt, sc_bmn, sc_cntn = _bias_tiles(rel_bias, past, dec, win_rows)
    cos_p, sin_p = _rope_tables(jnp.arange(seq, dtype=jnp.int32))
    cos_s, sin_s = _rope_tables(past + jnp.arange(dec, dtype=jnp.int32))
    cos_s, sin_s = jnp.tile(cos_s, (n_seq, 1)), jnp.tile(sin_s, (n_seq, 1))

    cache_kt = jnp.transpose(cache_a_k, (0, 1, 3, 4, 5, 2))
    cache_rt = jnp.transpose(cache_mla_krope, (0, 1, 3, 2))
    cache_v2 = cache_a_v.reshape(cache_a_v.shape[0], n_pool, PAGE_SIZE * KV_A, DV_A)
    hp = x_prompt.reshape(batch * seq, D_MODEL)
    hs = x_sample.reshape(n_seq * dec, D_MODEL)
    row2 = lambda a: a.reshape(1, -1)
    outs = {k: [] for k in ("ak_p", "ak_s", "av_p", "av_s", "ck_p", "ck_s", "kr_p", "kr_s", "wk_p", "wv_p",
                            "cv_p", "cv_s")}
    win_k = win_v = None

    for li in range(depth):
        i = li // 2
        lng = row2(ln_mix[li])
        if li % 2 == 0:
            lam_init = 0.8 - 0.6 * math.exp(-0.3 * li)
            w_in_aug, w_uq_aug, w_ukt, w_uvt = _prep_ab_weights(w_in_ab[i], w_uq[i], w_uk[i], w_uv[i])
            w_out = w_out_ab[i].astype(BF16)
            lam_vecs = (row2(lam_q1[i]), row2(lam_k1[i]), row2(lam_q2[i]), row2(lam_k2[i]))
            gh = row2(g_head_a[i])
            proj = lambda h, cos, sin, q_mult: _ab_proj(h, lng, w_in_aug, row2(g_cq[i]), w_uq_aug, w_ukt,
                                                        row2(g_ckv[i]), cos, sin, q_mult)
            qa, ka, va, ckv, kr, ka_bf, va_bf, kcat, qb = proj(hp, cos_p, sin_p, LOG2E)
            o_a = _flash_a(qa, ka_bf, va_bf, tiles_a, *lam_vecs, gh, batch, seq, lam_init)
            o_b = _flash_b(qb, kcat, w_uvt, batch, seq)
            hp = _out_proj(o_a, o_b, w_out, hp)
            outs["ak_p"].append(ka.reshape(batch, seq, KV_A, 2, DK_A))
            outs["av_p"].append(va.reshape(batch, seq, KV_A, DV_A))
            outs["ck_p"].append(ckv.reshape(batch, seq, KV_RANK))
            outs["kr_p"].append(kr.reshape(batch, seq, DR_B))
            qa, ka, va, ckv, kr, _, _, _, qb = proj(hs, cos_s, sin_s, 1.0)
            qa_s = jnp.transpose(qa.reshape(n_seq, dec, KV_A, G_A, 2, DK_A), (0, 2, 4, 3, 1, 5))
            qa_s = qa_s.reshape(n_seq, KV_A, 2, G_A * dec, DK_A)
            qb_s = jnp.transpose(qb.reshape(n_seq, dec, H_B, 256), (0, 2, 1, 3)).reshape(n_seq, H_B * dec, 256)
            knew = jnp.transpose(ka.reshape(n_seq, dec, KV_A, 2, DK_A), (0, 2, 3, 1, 4))
            vnew = jnp.transpose(va.reshape(n_seq, dec, KV_A, DV_A), (0, 2, 1, 3))
            kcn = jnp.concatenate([ckv, kr, jnp.zeros((n_seq * dec, 256 - KV_RANK - DR_B), F32)], axis=1)
            o_a, o_b = _sample_ab(i, page_table, qa_s, qb_s, knew, vnew, kcn.reshape(n_seq, dec, 256), bias_sa,
                                  *lam_vecs, gh, w_uvt, cache_kt, cache_v2, cache_mla_ckv, cache_rt, dec, lam_init)
            o_a = jnp.transpose(o_a.reshape(n_seq, KV_A, G_A, dec, DV_A), (0, 3, 1, 2, 4))
            o_b = jnp.transpose(o_b.reshape(n_seq, H_B, dec, DV_B), (0, 2, 1, 3))
            hs = _out_proj(o_a.reshape(n_seq * dec, H_A * DV_A), o_b.reshape(n_seq * dec, H_B * DV_B), w_out, hs)
            outs["ak_s"].append(ka.reshape(n_seq, dec, KV_A, 2, DK_A))
            outs["av_s"].append(va.reshape(n_seq, dec, KV_A, DV_A))
            outs["ck_s"].append(ckv.reshape(n_seq, dec, KV_RANK))
            outs["kr_s"].append(kr.reshape(n_seq, dec, DR_B))
        else:
            w_in = w_in_c[i].astype(BF16)
            w_out = w_out_c[i].astype(BF16)
            dils = tuple(dil for _, dil in DILATED if dil > 1)
            q, k, v, k_bf, v_bf, *cls = _c_proj(hp, lng, w_in, batch, seq, dils)
            by_class = lambda a: a.reshape(batch, 1, seq, H_C * DH_C)
            qkv = {1: (by_class(q), by_class(k_bf), by_class(v_bf))}
            for di, dil in enumerate(dils):
                qkv[dil] = tuple(cls[3 * di:3 * di + 3])
            branches = [_dilated_branch(*qkv[dil], tiles_c, b, batch, seq) for b, (_, dil) in enumerate(DILATED)]
            hp = _c_out([o for o, _ in branches], [l for _, l in branches], w_out, hp, seq)
            keep = min(WIN_MAX, seq)
            outs["wk_p"].append(k.reshape(batch, seq, H_C, DH_C)[:, seq - keep:])
            outs["wv_p"].append(v.reshape(batch, seq, H_C, DH_C)[:, seq - keep:])
            q, k, v = _c_proj(hs, lng, w_in)
            q_s = jnp.transpose(q.reshape(n_seq, dec, H_C, DH_C), (0, 2, 1, 3)).reshape(n_seq, H_C * dec, DH_C)
            o, win_k, win_v = _sample_c(i, q_s, k.reshape(n_seq, dec, H_C, DH_C), v.reshape(n_seq, dec, H_C, DH_C),
                                        state_win_k, state_win_v, win_k, win_v, sc_bm, sc_cnt, sc_bmn, sc_cntn, dec)
            o = jnp.transpose(o.reshape(n_seq, H_C, dec, DH_C), (0, 2, 1, 3)).reshape(n_seq * dec, H_C * DH_C)
            half = (H_C * DH_C) // 2
            hs = _out_proj(o[:, :half], o[:, half:], w_out, hs)
        w_up_bf = w_up[li].astype(BF16)
        w_down_bf = w_down[li].astype(BF16)
        last = li == depth - 1
        fg = row2(ln_final) if last else None
        hp, cst = _ffn(hp, row2(ln_ffn[li]), w_up_bf, conv_w[li], row2(conv_b[li]), w_down_bf, seq, final_gain=fg)
        tiles = seq // ROW_TILE
        outs["cv_p"].append(cst[tiles - 1::tiles, 8 - (CONV_W - 1):])
        prev = state_conv[li]
        zrow = jnp.zeros((n_seq, 1, D_FF), F32)
        s1 = jnp.concatenate([prev[:, 1:2]] + [zrow] * (dec - 1), axis=1).reshape(n_seq * dec, D_FF)
        s2 = jnp.concatenate([prev] + [zrow] * (dec - 2), axis=1).reshape(n_seq * dec, D_FF)
        hs, gate = _ffn(hs, row2(ln_ffn[li]), w_up_bf, conv_w[li], row2(conv_b[li]), w_down_bf, dec,
                        shifted=(s1, s2), final_gain=fg)
        outs["cv_s"].append(gate.reshape(n_seq, dec, D_FF)[:, dec - (CONV_W - 1):])

    st = lambda name: jnp.stack(outs[name])
    return (hp.reshape(batch, seq, D_MODEL), hs.reshape(n_seq, dec, D_MODEL),
            st("ak_p"), st("ak_s"), st("av_p"), st("av_s"), st("ck_p"), st("ck_s"), st("kr_p"), st("kr_s"),
            st("wk_p"), win_k, st("wv_p"), win_v, st("cv_p"), st("cv_s"))
```
